```python
import jax, jax.numpy as jnp
from jax import lax
import numpy as np

D_MODEL = 1024
BATCH = 2
SEQ = 8192
DEPTH = 2

N_MIXERS = 2
N_POOL_GROUPS = 4
POOL_WINDOWS = (2, 4, 8, 16)
POOL_GROUP_DIM = D_MODEL // N_POOL_GROUPS
CONV_WIDTH = 31
D_FF = 4 * D_MODEL
ALPHA = (2.0 * DEPTH) ** 0.25
BETA = (8.0 * DEPTH) ** -0.25
LN_EPS = 1e-5
N_POOL_LAYERS = (DEPTH + 1) // 2
N_CONV_LAYERS = DEPTH // 2

kernel_name = "hybrid_pool_conformer_sqrelu_deepnorm"


def layer_norm(x, g, b):
    xf = x.astype(jnp.float32)
    mu = jnp.mean(xf, axis=-1, keepdims=True)
    var = jnp.mean(jnp.square(xf - mu), axis=-1, keepdims=True)
    y = (xf - mu) * lax.rsqrt(var + LN_EPS)
    return (y * g.astype(jnp.float32) + b.astype(jnp.float32)).astype(x.dtype)


def pool_mixer(x, pool_w, pool_scale):
    B, S, _ = x.shape
    t = jnp.arange(S, dtype=jnp.float32)[None, :, None]
    outs = []
    for g, w in enumerate(POOL_WINDOWS):
        xg = x[..., g * POOL_GROUP_DIM:(g + 1) * POOL_GROUP_DIM]
        c = jnp.cumsum(xg.astype(jnp.float32), axis=1)
        c_pad = jnp.concatenate([jnp.zeros((B, 1, POOL_GROUP_DIM), jnp.float32), c], axis=1)
        hi = c_pad[:, 1:]
        lo = jnp.pad(c_pad[:, :S + 1 - w], ((0, 0), (w - 1, 0), (0, 0)))
        count = jnp.minimum(t + 1.0, float(w))
        d = ((hi - lo) / count).astype(x.dtype) - xg
        outs.append(jnp.einsum('bsc,cd->bsd', d, pool_w[g]))
    return jnp.concatenate(outs, axis=-1) * pool_scale


def conv_module(x, w_in, b_in, dw, dw_b, ln_g, ln_b, w_out, b_out):
    h = jnp.einsum('bsd,de->bse', x, w_in) + b_in
    a, gate = jnp.split(h, 2, axis=-1)
    h = a * jax.nn.sigmoid(gate)
    h = lax.conv_general_dilated(
        h, dw.reshape(CONV_WIDTH, 1, D_MODEL).astype(h.dtype),
        window_strides=(1,), padding=[(CONV_WIDTH - 1, 0)],
        dimension_numbers=('NWC', 'WIO', 'NWC'),
        feature_group_count=D_MODEL) + dw_b
    h = layer_norm(h, ln_g, ln_b)
    h = jax.nn.silu(h)
    return jnp.einsum('bsd,de->bse', h, w_out) + b_out


def sqrelu_mlp(x, w1, b1, w2, b2):
    h = jnp.einsum('bsd,df->bsf', x, w1) + b1
    h = jnp.square(jax.nn.relu(h))
    return jnp.einsum('bsf,fd->bsd', h, w2) + b2


def setup_inputs(seed: int = 0) -> dict:
    key = jax.random.key(seed)
    ks = jax.random.split(key, 24)
    D, F, G, Dg, K = D_MODEL, D_FF, N_POOL_GROUPS, POOL_GROUP_DIM, CONV_WIDTH
    nrm = jax.random.normal
    P, C, L = N_POOL_LAYERS, N_CONV_LAYERS, DEPTH
    return {
        "x": nrm(ks[0], (BATCH, SEQ, D), jnp.float32),
        "pool_w": nrm(ks[1], (P, G, Dg, Dg), jnp.float32) * (Dg ** -0.5) * BETA,
        "pool_scale": 1.0 + 0.1 * nrm(ks[2], (P, D), jnp.float32),
        "conv_w_in": nrm(ks[3], (C, D, 2 * D), jnp.float32) * (D ** -0.5),
        "conv_b_in": 0.02 * nrm(ks[4], (C, 2 * D), jnp.float32),
        "conv_dw": nrm(ks[5], (C, K, D), jnp.float32) * (K ** -0.5),
        "conv_dw_b": 0.02 * nrm(ks[6], (C, D), jnp.float32),
        "conv_ln_g": 1.0 + 0.05 * nrm(ks[7], (C, D), jnp.float32),
        "conv_ln_b": 0.02 * nrm(ks[8], (C, D), jnp.float32),
        "conv_w_out": nrm(ks[9], (C, D, D), jnp.float32) * (D ** -0.5) * BETA,
        "conv_b_out": 0.02 * nrm(ks[10], (C, D), jnp.float32),
        "mix_ln_g": 1.0 + 0.05 * nrm(ks[11], (L, D), jnp.float32),
        "mix_ln_b": 0.02 * nrm(ks[12], (L, D), jnp.float32),
        "mlp_w1": nrm(ks[13], (L, D, F), jnp.float32) * (D ** -0.5) * BETA,
        "mlp_b1": 0.02 * nrm(ks[14], (L, F), jnp.float32),
        "mlp_w2": nrm(ks[15], (L, F, D), jnp.float32) * (F ** -0.5) * BETA,
        "mlp_b2": 0.02 * nrm(ks[16], (L, D), jnp.float32),
        "mlp_ln_g": 1.0 + 0.05 * nrm(ks[17], (L, D), jnp.float32),
        "mlp_ln_b": 0.02 * nrm(ks[18], (L, D), jnp.float32),
    }


def reference(x, pool_w, pool_scale, conv_w_in, conv_b_in, conv_dw, conv_dw_b,
              conv_ln_g, conv_ln_b, conv_w_out, conv_b_out, mix_ln_g, mix_ln_b,
              mlp_w1, mlp_b1, mlp_w2, mlp_b2, mlp_ln_g, mlp_ln_b):
    for i in range(DEPTH):
        j = i // N_MIXERS
        if i % N_MIXERS == 0:
            mix = pool_mixer(x, pool_w[j], pool_scale[j])
        else:
            mix = conv_module(x, conv_w_in[j], conv_b_in[j], conv_dw[j], conv_dw_b[j],
                              conv_ln_g[j], conv_ln_b[j], conv_w_out[j], conv_b_out[j])
        x = layer_norm(ALPHA * x + mix, mix_ln_g[i], mix_ln_b[i])
        x = layer_norm(ALPHA * x + sqrelu_mlp(x, mlp_w1[i], mlp_b1[i], mlp_w2[i], mlp_b2[i]),
                       mlp_ln_g[i], mlp_ln_b[i])
    return x
```

```python
import functools

import jax
import jax.numpy as jnp
from jax.experimental import pallas as pl
from jax.experimental.pallas import tpu as pltpu

POOL_WINDOWS = (2, 4, 8, 16)
CONV_WIDTH = 31
DEPTH = 2
ALPHA = (2.0 * DEPTH) ** 0.25
LN_EPS = 1e-5

POOL_HALO = 16
CONV_HALO = 32

TOKEN_TILE = 512
VMEM_LIMIT_BYTES = 56 * 1024 * 1024


def _layer_norm(z, g, b):
    mu = jnp.mean(z, axis=-1, keepdims=True)
    zc = z - mu
    var = jnp.mean(zc * zc, axis=-1, keepdims=True)
    return zc * jax.lax.rsqrt(var + LN_EPS) * g + b


def _pool_kernel(x_ref, w_ref, scale_ref, g_ref, b_ref, o_ref, buf_ref):
    i = pl.program_id(1)
    tm = x_ref.shape[1]
    dg = w_ref.shape[1]

    @pl.when(i == 0)
    def _():
        buf_ref[0:POOL_HALO, :] = jnp.zeros((POOL_HALO, buf_ref.shape[1]), jnp.float32)

    x = x_ref[0]
    buf_ref[POOL_HALO:, :] = x
    t = i * tm + jax.lax.broadcasted_iota(jnp.int32, (tm, 1), 0)
    tf = (t + 1).astype(jnp.float32)
    for g, w in enumerate(POOL_WINDOWS):
        cols = slice(g * dg, (g + 1) * dg)
        xg = x[:, cols]
        s = xg
        for k in range(1, w):
            s = s + buf_ref[POOL_HALO - k:POOL_HALO - k + tm, cols]
        count = jnp.minimum(tf, float(w))
        d = s / count - xg
        mix = jnp.dot(d.astype(jnp.bfloat16), w_ref[g], preferred_element_type=jnp.float32)
        z = ALPHA * xg + mix * scale_ref[:, cols]
        o_ref[0, :, cols] = z
    buf_ref[0:POOL_HALO, :] = buf_ref[tm:tm + POOL_HALO, :]
    o_ref[0] = _layer_norm(o_ref[0], g_ref[...], b_ref[...])


def _mlp_kernel(x_ref, w1_ref, b1_ref, w2_ref, b2_ref, g_ref, b_ref, o_ref, h_ref):
    x = x_ref[...]
    h = jnp.dot(x.astype(jnp.bfloat16), w1_ref[...], preferred_element_type=jnp.float32) + b1_ref[...]
    h = jnp.maximum(h, 0.0)
    h_ref[...] = (h * h).astype(jnp.bfloat16)
    y = jnp.dot(h_ref[...], w2_ref[...], preferred_element_type=jnp.float32) + b2_ref[...]
    o_ref[...] = _layer_norm(ALPHA * x + y, g_ref[...], b_ref[...])


def _conv_kernel(x_ref, win_ref, bin_ref, dw_ref, dwb_ref, lng_ref, lnb_ref, wout_ref, bout_ref,
                 g_ref, b_ref, o_ref, buf_ref):
    i = pl.program_id(1)
    tm = x_ref.shape[1]
    d = x_ref.shape[2]

    @pl.when(i == 0)
    def _():
        buf_ref[0:CONV_HALO, :] = jnp.zeros((CONV_HALO, d), jnp.float32)

    x = x_ref[0]
    h = jnp.dot(x.astype(jnp.bfloat16), win_ref[...], preferred_element_type=jnp.float32) + bin_ref[...]
    buf_ref[CONV_HALO:, :] = h[:, :d] * jax.nn.sigmoid(h[:, d:])
    base = CONV_HALO - (CONV_WIDTH - 1)
    acc = buf_ref[base:base + tm, :] * dw_ref[0:1, :]
    for k in range(1, CONV_WIDTH):
        acc = acc + buf_ref[base + k:base + k + tm, :] * dw_ref[k:k + 1, :]
    acc = acc + dwb_ref[...]
    buf_ref[0:CONV_HALO, :] = buf_ref[tm:tm + CONV_HALO, :]
    c = _layer_norm(acc, lng_ref[...], lnb_ref[...])
    c = c * jax.nn.sigmoid(c)
    y = jnp.dot(c.astype(jnp.bfloat16), wout_ref[...], preferred_element_type=jnp.float32) + bout_ref[...]
    o_ref[0] = _layer_norm(ALPHA * x + y, g_ref[...], b_ref[...])


def _resident(shape):
    nd = len(shape)
    return pl.BlockSpec(shape, lambda *_: (0,) * nd, pipeline_mode=pl.Buffered(1))


def _row(v):
    return v.reshape(1, -1)


def _pool_layer(x, pool_w, pool_scale, ln_g, ln_b):
    b, s, d = x.shape
    tm = TOKEN_TILE
    tile = pl.BlockSpec((1, tm, d), lambda bi, i: (bi, i, 0))
    return pl.pallas_call(
        _pool_kernel,
        grid=(b, s // tm),
        in_specs=[tile, _resident(pool_w.shape), _resident((1, d)), _resident((1, d)), _resident((1, d))],
        out_specs=tile,
        out_shape=jax.ShapeDtypeStruct(x.shape, x.dtype),
        scratch_shapes=[pltpu.VMEM((POOL_HALO + tm, d), jnp.float32)],
        compiler_params=pltpu.CompilerParams(
            dimension_semantics=("arbitrary", "arbitrary"), vmem_limit_bytes=VMEM_LIMIT_BYTES),
        name="pool_mixer",
    )(x, pool_w.astype(jnp.bfloat16), _row(pool_scale), _row(ln_g), _row(ln_b))


def _mlp_layer(x, w1, b1, w2, b2, ln_g, ln_b):
    b, s, d = x.shape
    f = w1.shape[1]
    tm = TOKEN_TILE
    n = b * s
    tile = pl.BlockSpec((tm, d), lambda i: (i, 0))
    out = pl.pallas_call(
        _mlp_kernel,
        grid=(n // tm,),
        in_specs=[tile, _resident((d, f)), _resident((1, f)), _resident((f, d)), _resident((1, d)),
                  _resident((1, d)), _resident((1, d))],
        out_specs=tile,
        out_shape=jax.ShapeDtypeStruct((n, d), x.dtype),
        scratch_shapes=[pltpu.VMEM((tm, f), jnp.bfloat16)],
        compiler_params=pltpu.CompilerParams(
            dimension_semantics=("arbitrary",), vmem_limit_bytes=VMEM_LIMIT_BYTES),
        name="sqrelu_mlp",
    )(x.reshape(n, d), w1.astype(jnp.bfloat16), _row(b1), w2.astype(jnp.bfloat16), _row(b2),
      _row(ln_g), _row(ln_b))
    return out.reshape(b, s, d)


def _conv_layer(x, w_in, b_in, dw, dw_b, ln_g, ln_b, w_out, b_out, mix_g, mix_b):
    b, s, d = x.shape
    tm = TOKEN_TILE
    tile = pl.BlockSpec((1, tm, d), lambda bi, i: (bi, i, 0))
    return pl.pallas_call(
        _conv_kernel,
        grid=(b, s // tm),
        in_specs=[tile, _resident((d, 2 * d)), _resident((1, 2 * d)), _resident(dw.shape), _resident((1, d)),
                  _resident((1, d)), _resident((1, d)), _resident((d, d)), _resident((1, d)),
                  _resident((1, d)), _resident((1, d))],
        out_specs=tile,
        out_shape=jax.ShapeDtypeStruct(x.shape, x.dtype),
        scratch_shapes=[pltpu.VMEM((CONV_HALO + tm, d), jnp.float32)],
        compiler_params=pltpu.CompilerParams(
            dimension_semantics=("arbitrary", "arbitrary"), vmem_limit_bytes=VMEM_LIMIT_BYTES),
        name="conv_module",
    )(x, w_in.astype(jnp.bfloat16), _row(b_in), dw, _row(dw_b), _row(ln_g), _row(ln_b),
      w_out.astype(jnp.bfloat16), _row(b_out), _row(mix_g), _row(mix_b))


def kernel(x, pool_w, pool_scale, conv_w_in, conv_b_in, conv_dw, conv_dw_b, conv_ln_g, conv_ln_b,
           conv_w_out, conv_b_out, mix_ln_g, mix_ln_b, mlp_w1, mlp_b1, mlp_w2, mlp_b2, mlp_ln_g, mlp_ln_b):
    for i in range(DEPTH):
        j = i // 2
        if i % 2 == 0:
            x = _pool_layer(x, pool_w[j], pool_scale[j], mix_ln_g[i], mix_ln_b[i])
        else:
            x = _conv_layer(x, conv_w_in[j], conv_b_in[j], conv_dw[j], conv_dw_b[j], conv_ln_g[j],
                            conv_ln_b[j], conv_w_out[j], conv_b_out[j], mix_ln_g[i], mix_ln_b[i])
        x = _mlp_layer(x, mlp_w1[i], mlp_b1[i], mlp_w2[i], mlp_b2[i], mlp_ln_g[i], mlp_ln_b[i])
    return x
```

```python
import functools

import jax
import jax.numpy as jnp
from jax.experimental import pallas as pl
from jax.experimental.pallas import tpu as pltpu

POOL_WINDOWS = (2, 4, 8, 16)
CONV_WIDTH = 31
DEPTH = 2
ALPHA = (2.0 * DEPTH) ** 0.25
LN_EPS = 1e-5

POOL_HALO = 16
CONV_HALO = 32
LANES = 128

TOKEN_TILE = 512
VMEM_LIMIT_BYTES = 56 * 1024 * 1024


def _layer_norm(z, g, b):
    mu = jnp.mean(z, axis=-1, keepdims=True)
    zc = z - mu
    var = jnp.mean(zc * zc, axis=-1, keepdims=True)
    return zc * jax.lax.rsqrt(var + LN_EPS) * g + b


def _pool_kernel(x_ref, w_ref, scale_ref, g_ref, b_ref, o_ref, buf_ref):
    i = pl.program_id(1)
    tm = x_ref.shape[1]
    dg = w_ref.shape[1]
    n_slabs = buf_ref.shape[0]
    slabs_per_group = dg // LANES

    @pl.when(i == 0)
    def _():
        buf_ref[:, 0:POOL_HALO, :] = jnp.zeros((n_slabs, POOL_HALO, LANES), jnp.float32)

    x = x_ref[0]
    for l in range(n_slabs):
        buf_ref[l, POOL_HALO:, :] = x[:, l * LANES:(l + 1) * LANES]
    t = i * tm + jax.lax.broadcasted_iota(jnp.int32, (tm, 1), 0)
    tf = (t + 1).astype(jnp.float32)
    for g, w in enumerate(POOL_WINDOWS):
        cols = slice(g * dg, (g + 1) * dg)
        xg = x[:, cols]
        parts = []
        for l in range(g * slabs_per_group, (g + 1) * slabs_per_group):
            s = buf_ref[l, POOL_HALO:POOL_HALO + tm, :]
            for k in range(1, w):
                s = s + buf_ref[l, POOL_HALO - k:POOL_HALO - k + tm, :]
            parts.append(s)
        s = jnp.concatenate(parts, axis=-1)
        count = jnp.minimum(tf, float(w))
        d = s / count - xg
        mix = jnp.dot(d.astype(jnp.bfloat16), w_ref[g], preferred_element_type=jnp.float32)
        z = ALPHA * xg + mix * scale_ref[:, cols]
        o_ref[0, :, cols] = z
    buf_ref[:, 0:POOL_HALO, :] = buf_ref[:, tm:tm + POOL_HALO, :]
    o_ref[0] = _layer_norm(o_ref[0], g_ref[...], b_ref[...])


def _mlp_kernel(x_ref, w1_ref, b1_ref, w2_ref, b2_ref, g_ref, b_ref, o_ref, h_ref):
    x = x_ref[...]
    h = jnp.dot(x.astype(jnp.bfloat16), w1_ref[...], preferred_element_type=jnp.float32) + b1_ref[...]
    h = jnp.maximum(h, 0.0)
    h_ref[...] = (h * h).astype(jnp.bfloat16)
    y = jnp.dot(h_ref[...], w2_ref[...], preferred_element_type=jnp.float32) + b2_ref[...]
    o_ref[...] = _layer_norm(ALPHA * x + y, g_ref[...], b_ref[...])


def _conv_kernel(x_ref, win_ref, bin_ref, dw_ref, dwb_ref, lng_ref, lnb_ref, wout_ref, bout_ref,
                 g_ref, b_ref, o_ref, buf_ref):
    i = pl.program_id(1)
    tm = x_ref.shape[1]
    d = x_ref.shape[2]
    n_slabs = buf_ref.shape[0]

    @pl.when(i == 0)
    def _():
        buf_ref[:, 0:CONV_HALO, :] = jnp.zeros((n_slabs, CONV_HALO, LANES), jnp.float32)

    x = x_ref[0]
    h = jnp.dot(x.astype(jnp.bfloat16), win_ref[...], preferred_element_type=jnp.float32) + bin_ref[...]
    glu = h[:, :d] * jax.nn.sigmoid(h[:, d:])
    for l in range(n_slabs):
        buf_ref[l, CONV_HALO:, :] = glu[:, l * LANES:(l + 1) * LANES]
    base = CONV_HALO - (CONV_WIDTH - 1)
    for l in range(n_slabs):
        lanes = slice(l * LANES, (l + 1) * LANES)
        acc = buf_ref[l, base:base + tm, :] * dw_ref[0:1, lanes]
        for k in range(1, CONV_WIDTH):
            acc = acc + buf_ref[l, base + k:base + k + tm, :] * dw_ref[k:k + 1, lanes]
        o_ref[0, :, lanes] = acc + dwb_ref[:, lanes]
    buf_ref[:, 0:CONV_HALO, :] = buf_ref[:, tm:tm + CONV_HALO, :]
    c = _layer_norm(o_ref[0], lng_ref[...], lnb_ref[...])
    c = c * jax.nn.sigmoid(c)
    y = jnp.dot(c.astype(jnp.bfloat16), wout_ref[...], preferred_element_type=jnp.float32) + bout_ref[...]
    o_ref[0] = _layer_norm(ALPHA * x + y, g_ref[...], b_ref[...])


def _resident(shape):
    nd = len(shape)
    return pl.BlockSpec(shape, lambda *_: (0,) * nd, pipeline_mode=pl.Buffered(1))


def _row(v):
    return v.reshape(1, -1)


def _pool_layer(x, pool_w, pool_scale, ln_g, ln_b):
    b, s, d = x.shape
    tm = TOKEN_TILE
    tile = pl.BlockSpec((1, tm, d), lambda bi, i: (bi, i, 0))
    return pl.pallas_call(
        _pool_kernel,
        grid=(b, s // tm),
        in_specs=[tile, _resident(pool_w.shape), _resident((1, d)), _resident((1, d)), _resident((1, d))],
        out_specs=tile,
        out_shape=jax.ShapeDtypeStruct(x.shape, x.dtype),
        scratch_shapes=[pltpu.VMEM((d // LANES, POOL_HALO + tm, LANES), jnp.float32)],
        compiler_params=pltpu.CompilerParams(
            dimension_semantics=("arbitrary", "arbitrary"), vmem_limit_bytes=VMEM_LIMIT_BYTES),
        name="pool_mixer",
    )(x, pool_w.astype(jnp.bfloat16), _row(pool_scale), _row(ln_g), _row(ln_b))


def _mlp_layer(x, w1, b1, w2, b2, ln_g, ln_b):
    b, s, d = x.shape
    f = w1.shape[1]
    tm = TOKEN_TILE
    n = b * s
    tile = pl.BlockSpec((tm, d), lambda i: (i, 0))
    out = pl.pallas_call(
        _mlp_kernel,
        grid=(n // tm,),
        in_specs=[tile, _resident((d, f)), _resident((1, f)), _resident((f, d)), _resident((1, d)),
                  _resident((1, d)), _resident((1, d))],
        out_specs=tile,
        out_shape=jax.ShapeDtypeStruct((n, d), x.dtype),
        scratch_shapes=[pltpu.VMEM((tm, f), jnp.bfloat16)],
        compiler_params=pltpu.CompilerParams(
            dimension_semantics=("arbitrary",), vmem_limit_bytes=VMEM_LIMIT_BYTES),
        name="sqrelu_mlp",
    )(x.reshape(n, d), w1.astype(jnp.bfloat16), _row(b1), w2.astype(jnp.bfloat16), _row(b2),
      _row(ln_g), _row(ln_b))
    return out.reshape(b, s, d)


def _conv_layer(x, w_in, b_in, dw, dw_b, ln_g, ln_b, w_out, b_out, mix_g, mix_b):
    b, s, d = x.shape
    tm = TOKEN_TILE
    tile = pl.BlockSpec((1, tm, d), lambda bi, i: (bi, i, 0))
    return pl.pallas_call(
        _conv_kernel,
        grid=(b, s // tm),
        in_specs=[tile, _resident((d, 2 * d)), _resident((1, 2 * d)), _resident(dw.shape), _resident((1, d)),
                  _resident((1, d)), _resident((1, d)), _resident((d, d)), _resident((1, d)),
                  _resident((1, d)), _resident((1, d))],
        out_specs=tile,
        out_shape=jax.ShapeDtypeStruct(x.shape, x.dtype),
        scratch_shapes=[pltpu.VMEM((d // LANES, CONV_HALO + tm, LANES), jnp.float32)],
        compiler_params=pltpu.CompilerParams(
            dimension_semantics=("arbitrary", "arbitrary"), vmem_limit_bytes=VMEM_LIMIT_BYTES),
        name="conv_module",
    )(x, w_in.astype(jnp.bfloat16), _row(b_in), dw, _row(dw_b), _row(ln_g), _row(ln_b),
      w_out.astype(jnp.bfloat16), _row(b_out), _row(mix_g), _row(mix_b))


def kernel(x, pool_w, pool_scale, conv_w_in, conv_b_in, conv_dw, conv_dw_b, conv_ln_g, conv_ln_b,
           conv_w_out, conv_b_out, mix_ln_g, mix_ln_b, mlp_w1, mlp_b1, mlp_w2, mlp_b2, mlp_ln_g, mlp_ln_b):
    for i in range(DEPTH):
        j = i // 2
        if i % 2 == 0:
            x = _pool_layer(x, pool_w[j], pool_scale[j], mix_ln_g[i], mix_ln_b[i])
        else:
            x = _conv_layer(x, conv_w_in[j], conv_b_in[j], conv_dw[j], conv_dw_b[j], conv_ln_g[j],
                            conv_ln_b[j], conv_w_out[j], conv_b_out[j], mix_ln_g[i], mix_ln_b[i])
        x = _mlp_layer(x, mlp_w1[i], mlp_b1[i], mlp_w2[i], mlp_b2[i], mlp_ln_g[i], mlp_ln_b[i])
    return x
```

```python
import jax
import jax.numpy as jnp
from jax.experimental import pallas as pl
from jax.experimental.pallas import tpu as pltpu

POOL_WINDOWS = (2, 4, 8, 16)
CONV_WIDTH = 31
DEPTH = 2
ALPHA = (2.0 * DEPTH) ** 0.25
LN_EPS = 1e-5

POOL_HALO = 16
CONV_HALO = 32
LANES = 128

TOKEN_TILE = 512
MLP_TOKEN_TILE = 1024
MLP_ROWS = 256
CONV_ROWS = 128
TAP_ROWS = 64
VMEM_LIMIT_BYTES = 56 * 1024 * 1024


def _layer_norm(z, g, b):
    mu = jnp.mean(z, axis=-1, keepdims=True)
    zc = z - mu
    var = jnp.mean(zc * zc, axis=-1, keepdims=True)
    return zc * jax.lax.rsqrt(var + LN_EPS) * g + b


def _pool_kernel(x_ref, w_ref, scale_ref, g_ref, b_ref, o_ref, buf_ref):
    i = pl.program_id(1)
    tm = x_ref.shape[1]
    dg = w_ref.shape[1]
    n_slabs = buf_ref.shape[0]
    slabs_per_group = dg // LANES

    @pl.when(i == 0)
    def _():
        buf_ref[:, 0:POOL_HALO, :] = jnp.zeros((n_slabs, POOL_HALO, LANES), jnp.float32)

    x = x_ref[0]
    for l in range(n_slabs):
        buf_ref[l, POOL_HALO:, :] = x[:, l * LANES:(l + 1) * LANES]
    t = i * tm + jax.lax.broadcasted_iota(jnp.int32, (tm, 1), 0)
    tf = (t + 1).astype(jnp.float32)
    for g, w in enumerate(POOL_WINDOWS):
        cols = slice(g * dg, (g + 1) * dg)
        xg = x[:, cols]
        parts = []
        for l in range(g * slabs_per_group, (g + 1) * slabs_per_group):
            s = buf_ref[l, POOL_HALO:POOL_HALO + tm, :]
            for k in range(1, w):
                s = s + buf_ref[l, POOL_HALO - k:POOL_HALO - k + tm, :]
            parts.append(s)
        s = jnp.concatenate(parts, axis=-1)
        count = jnp.minimum(tf, float(w))
        d = s / count - xg
        mix = jnp.dot(d.astype(jnp.bfloat16), w_ref[g], preferred_element_type=jnp.float32)
        z = ALPHA * xg + mix * scale_ref[:, cols]
        o_ref[0, :, cols] = z
    buf_ref[:, 0:POOL_HALO, :] = buf_ref[:, tm:tm + POOL_HALO, :]
    o_ref[0] = _layer_norm(o_ref[0], g_ref[...], b_ref[...])


def _mlp_kernel(x_ref, w1_ref, b1_ref, w2_ref, b2_ref, g_ref, b_ref, o_ref, h_ref):
    tm = x_ref.shape[0]
    for r0 in range(0, tm, MLP_ROWS):
        rows = slice(r0, r0 + MLP_ROWS)
        xb = x_ref[rows, :].astype(jnp.bfloat16)
        h = jnp.dot(xb, w1_ref[...], preferred_element_type=jnp.float32) + b1_ref[...]
        h = jnp.maximum(h, 0.0)
        h_ref[rows, :] = (h * h).astype(jnp.bfloat16)
    for r0 in range(0, tm, MLP_ROWS):
        rows = slice(r0, r0 + MLP_ROWS)
        y = jnp.dot(h_ref[rows, :], w2_ref[...], preferred_element_type=jnp.float32) + b2_ref[...]
        o_ref[rows, :] = _layer_norm(ALPHA * x_ref[rows, :] + y, g_ref[...], b_ref[...])


def _conv_kernel(x_ref, win_ref, bin_ref, dw_ref, dwb_ref, lng_ref, lnb_ref, wout_ref, bout_ref,
                 g_ref, b_ref, o_ref, buf_ref):
    i = pl.program_id(1)
    tm = x_ref.shape[1]
    d = x_ref.shape[2]
    n_slabs = buf_ref.shape[0]
    base = CONV_HALO - (CONV_WIDTH - 1)

    @pl.when(i == 0)
    def _():
        buf_ref[:, 0:CONV_HALO, :] = jnp.zeros((n_slabs, CONV_HALO, LANES), jnp.float32)

    def glu_to_slabs(c):
        rows = slice(c * CONV_ROWS, (c + 1) * CONV_ROWS)
        xb = x_ref[0, rows, :].astype(jnp.bfloat16)
        h = jnp.dot(xb, win_ref[...], preferred_element_type=jnp.float32) + bin_ref[...]
        glu = h[:, :d] * jax.nn.sigmoid(h[:, d:])
        for l in range(n_slabs):
            buf_ref[l, CONV_HALO + c * CONV_ROWS:CONV_HALO + (c + 1) * CONV_ROWS, :] = (
                glu[:, l * LANES:(l + 1) * LANES])

    def depthwise_taps(c):
        for l in range(n_slabs):
            lanes = slice(l * LANES, (l + 1) * LANES)
            for r0 in range(c * CONV_ROWS, (c + 1) * CONV_ROWS, TAP_ROWS):
                acc = buf_ref[l, base + r0:base + r0 + TAP_ROWS, :] * dw_ref[0:1, lanes]
                for k in range(1, CONV_WIDTH):
                    acc = acc + buf_ref[l, base + r0 + k:base + r0 + k + TAP_ROWS, :] * dw_ref[k:k + 1, lanes]
                o_ref[0, r0:r0 + TAP_ROWS, lanes] = acc + dwb_ref[:, lanes]

    def tail(c):
        rows = slice(c * CONV_ROWS, (c + 1) * CONV_ROWS)
        cv = _layer_norm(o_ref[0, rows, :], lng_ref[...], lnb_ref[...])
        cv = cv * jax.nn.sigmoid(cv)
        y = jnp.dot(cv.astype(jnp.bfloat16), wout_ref[...], preferred_element_type=jnp.float32) + bout_ref[...]
        o_ref[0, rows, :] = _layer_norm(ALPHA * x_ref[0, rows, :] + y, g_ref[...], b_ref[...])

    n_chunks = tm // CONV_ROWS
    glu_to_slabs(0)
    for c in range(n_chunks):
        if c + 1 < n_chunks:
            glu_to_slabs(c + 1)
        depthwise_taps(c)
        tail(c)
    buf_ref[:, 0:CONV_HALO, :] = buf_ref[:, tm:tm + CONV_HALO, :]


def _resident(shape):
    nd = len(shape)
    return pl.BlockSpec(shape, lambda *_: (0,) * nd, pipeline_mode=pl.Buffered(1))


def _row(v):
    return v.reshape(1, -1)


def _pool_layer(x, pool_w, pool_scale, ln_g, ln_b):
    b, s, d = x.shape
    tm = TOKEN_TILE
    tile = pl.BlockSpec((1, tm, d), lambda bi, i: (bi, i, 0))
    return pl.pallas_call(
        _pool_kernel,
        grid=(b, s // tm),
        in_specs=[tile, _resident(pool_w.shape), _resident((1, d)), _resident((1, d)), _resident((1, d))],
        out_specs=tile,
        out_shape=jax.ShapeDtypeStruct(x.shape, x.dtype),
        scratch_shapes=[pltpu.VMEM((d // LANES, POOL_HALO + tm, LANES), jnp.float32)],
        compiler_params=pltpu.CompilerParams(
            dimension_semantics=("arbitrary", "arbitrary"), vmem_limit_bytes=VMEM_LIMIT_BYTES),
        name="pool_mixer",
    )(x, pool_w.astype(jnp.bfloat16), _row(pool_scale), _row(ln_g), _row(ln_b))


def _mlp_layer(x, w1, b1, w2, b2, ln_g, ln_b):
    b, s, d = x.shape
    f = w1.shape[1]
    tm = MLP_TOKEN_TILE
    n = b * s
    tile = pl.BlockSpec((tm, d), lambda i: (i, 0))
    out = pl.pallas_call(
        _mlp_kernel,
        grid=(n // tm,),
        in_specs=[tile, _resident((d, f)), _resident((1, f)), _resident((f, d)), _resident((1, d)),
                  _resident((1, d)), _resident((1, d))],
        out_specs=tile,
        out_shape=jax.ShapeDtypeStruct((n, d), x.dtype),
        scratch_shapes=[pltpu.VMEM((tm, f), jnp.bfloat16)],
        compiler_params=pltpu.CompilerParams(
            dimension_semantics=("arbitrary",), vmem_limit_bytes=VMEM_LIMIT_BYTES),
        name="sqrelu_mlp",
    )(x.reshape(n, d), w1.astype(jnp.bfloat16), _row(b1), w2.astype(jnp.bfloat16), _row(b2),
      _row(ln_g), _row(ln_b))
    return out.reshape(b, s, d)


def _conv_layer(x, w_in, b_in, dw, dw_b, ln_g, ln_b, w_out, b_out, mix_g, mix_b):
    b, s, d = x.shape
    tm = TOKEN_TILE
    tile = pl.BlockSpec((1, tm, d), lambda bi, i: (bi, i, 0))
    return pl.pallas_call(
        _conv_kernel,
        grid=(b, s // tm),
        in_specs=[tile, _resident((d, 2 * d)), _resident((1, 2 * d)), _resident(dw.shape), _resident((1, d)),
                  _resident((1, d)), _resident((1, d)), _resident((d, d)), _resident((1, d)),
                  _resident((1, d)), _resident((1, d))],
        out_specs=tile,
        out_shape=jax.ShapeDtypeStruct(x.shape, x.dtype),
        scratch_shapes=[pltpu.VMEM((d // LANES, CONV_HALO + tm, LANES), jnp.float32)],
        compiler_params=pltpu.CompilerParams(
            dimension_semantics=("arbitrary", "arbitrary"), vmem_limit_bytes=VMEM_LIMIT_BYTES),
        name="conv_module",
    )(x, w_in.astype(jnp.bfloat16), _row(b_in), dw, _row(dw_b), _row(ln_g), _row(ln_b),
      w_out.astype(jnp.bfloat16), _row(b_out), _row(mix_g), _row(mix_b))


def kernel(x, pool_w, pool_scale, conv_w_in, conv_b_in, conv_dw, conv_dw_b, conv_ln_g, conv_ln_b,
           conv_w_out, conv_b_out, mix_ln_g, mix_ln_b, mlp_w1, mlp_b1, mlp_w2, mlp_b2, mlp_ln_g, mlp_ln_b):
    for i in range(DEPTH):
        j = i // 2
        if i % 2 == 0:
            x = _pool_layer(x, pool_w[j], pool_scale[j], mix_ln_g[i], mix_ln_b[i])
        else:
            x = _conv_layer(x, conv_w_in[j], conv_b_in[j], conv_dw[j], conv_dw_b[j], conv_ln_g[j],
                            conv_ln_b[j], conv_w_out[j], conv_b_out[j], mix_ln_g[i], mix_ln_b[i])
        x = _mlp_layer(x, mlp_w1[i], mlp_b1[i], mlp_w2[i], mlp_b2[i], mlp_ln_g[i], mlp_ln_b[i])
    return x
```

```python
import jax
import jax.numpy as jnp
from jax.experimental import pallas as pl
from jax.experimental.pallas import tpu as pltpu

POOL_WINDOWS = (2, 4, 8, 16)
CONV_WIDTH = 31
DEPTH = 2
ALPHA = (2.0 * DEPTH) ** 0.25
LN_EPS = 1e-5

POOL_HALO = 16
CONV_HALO = 32
LANES = 128

TOKEN_TILE = 512
MLP_ROWS = 256
VMEM_LIMIT_BYTES = 56 * 1024 * 1024


def _layer_norm(z, g, b):
    mu = jnp.mean(z, axis=-1, keepdims=True)
    zc = z - mu
    var = jnp.mean(zc * zc, axis=-1, keepdims=True)
    return zc * jax.lax.rsqrt(var + LN_EPS) * g + b


def _dot(a, b):
    return jnp.dot(a, b, preferred_element_type=jnp.float32)


def _pool_kernel(x_ref, w_ref, scale_ref, g_ref, b_ref, o_ref, buf_ref):
    i = pl.program_id(1)
    tm = x_ref.shape[1]
    dg = w_ref.shape[1]
    n_slabs = buf_ref.shape[0]
    slabs_per_group = dg // LANES

    @pl.when(i == 0)
    def _():
        buf_ref[:, 0:POOL_HALO, :] = jnp.zeros((n_slabs, POOL_HALO, LANES), jnp.float32)

    x = x_ref[0]
    for l in range(n_slabs):
        buf_ref[l, POOL_HALO:, :] = x[:, l * LANES:(l + 1) * LANES]
    t = i * tm + jax.lax.broadcasted_iota(jnp.int32, (tm, 1), 0)
    tf = (t + 1).astype(jnp.float32)
    for g, w in enumerate(POOL_WINDOWS):
        cols = slice(g * dg, (g + 1) * dg)
        xg = x[:, cols]
        parts = []
        for l in range(g * slabs_per_group, (g + 1) * slabs_per_group):
            s = buf_ref[l, POOL_HALO:POOL_HALO + tm, :]
            for k in range(1, w):
                s = s + buf_ref[l, POOL_HALO - k:POOL_HALO - k + tm, :]
            parts.append(s)
        s = jnp.concatenate(parts, axis=-1)
        count = jnp.minimum(tf, float(w))
        d = s / count - xg
        z = ALPHA * xg + _dot(d, w_ref[g]) * scale_ref[:, cols]
        o_ref[0, :, cols] = z
    buf_ref[:, 0:POOL_HALO, :] = buf_ref[:, tm:tm + POOL_HALO, :]
    o_ref[0] = _layer_norm(o_ref[0], g_ref[...], b_ref[...])


def _mlp_kernel(x_ref, w1_ref, b1_ref, w2_ref, b2_ref, g_ref, b_ref, o_ref, h_ref):
    tm = x_ref.shape[0]
    for r0 in range(0, tm, MLP_ROWS):
        rows = slice(r0, r0 + MLP_ROWS)
        h = jnp.maximum(_dot(x_ref[rows, :], w1_ref[...]) + b1_ref[...], 0.0)
        h_ref[rows, :] = h * h
    for r0 in range(0, tm, MLP_ROWS):
        rows = slice(r0, r0 + MLP_ROWS)
        y = _dot(h_ref[rows, :], w2_ref[...]) + b2_ref[...]
        o_ref[rows, :] = _layer_norm(ALPHA * x_ref[rows, :] + y, g_ref[...], b_ref[...])


def _conv_kernel(x_ref, win_ref, bin_ref, dw_ref, dwb_ref, lng_ref, lnb_ref, wout_ref, bout_ref,
                 g_ref, b_ref, o_ref, buf_ref):
    i = pl.program_id(1)
    tm = x_ref.shape[1]
    d = x_ref.shape[2]
    n_slabs = buf_ref.shape[0]

    @pl.when(i == 0)
    def _():
        buf_ref[:, 0:CONV_HALO, :] = jnp.zeros((n_slabs, CONV_HALO, LANES), jnp.float32)

    x = x_ref[0]
    h = _dot(x, win_ref[...]) + bin_ref[...]
    glu = h[:, :d] * jax.nn.sigmoid(h[:, d:])
    for l in range(n_slabs):
        buf_ref[l, CONV_HALO:, :] = glu[:, l * LANES:(l + 1) * LANES]
    base = CONV_HALO - (CONV_WIDTH - 1)
    for l in range(n_slabs):
        lanes = slice(l * LANES, (l + 1) * LANES)
        acc = buf_ref[l, base:base + tm, :] * dw_ref[0:1, lanes]
        for k in range(1, CONV_WIDTH):
            acc = acc + buf_ref[l, base + k:base + k + tm, :] * dw_ref[k:k + 1, lanes]
        o_ref[0, :, lanes] = acc + dwb_ref[:, lanes]
    buf_ref[:, 0:CONV_HALO, :] = buf_ref[:, tm:tm + CONV_HALO, :]
    c = _layer_norm(o_ref[0], lng_ref[...], lnb_ref[...])
    c = c * jax.nn.sigmoid(c)
    y = _dot(c, wout_ref[...]) + bout_ref[...]
    o_ref[0] = _layer_norm(ALPHA * x + y, g_ref[...], b_ref[...])


def _layer_block(layer, shape):
    nd = len(shape)
    return pl.BlockSpec((None,) + tuple(shape), lambda *_: (layer,) + (0,) * nd,
                        pipeline_mode=pl.Buffered(1))


def _rows(v):
    return v.reshape(v.shape[0], 1, v.shape[1])


def _params(*dims):
    return pltpu.CompilerParams(dimension_semantics=dims, vmem_limit_bytes=VMEM_LIMIT_BYTES)


def _pool_layer(x, j, i, pool_w, pool_scale, ln_g, ln_b):
    b, s, d = x.shape
    tm = TOKEN_TILE
    tile = pl.BlockSpec((1, tm, d), lambda bi, ti: (bi, ti, 0))
    return pl.pallas_call(
        _pool_kernel,
        grid=(b, s // tm),
        in_specs=[tile, _layer_block(j, pool_w.shape[1:]), _layer_block(j, (1, d)),
                  _layer_block(i, (1, d)), _layer_block(i, (1, d))],
        out_specs=tile,
        out_shape=jax.ShapeDtypeStruct(x.shape, x.dtype),
        scratch_shapes=[pltpu.VMEM((d // LANES, POOL_HALO + tm, LANES), jnp.float32)],
        compiler_params=_params("arbitrary", "arbitrary"),
        name="pool_mixer",
    )(x, pool_w, _rows(pool_scale), _rows(ln_g), _rows(ln_b))


def _mlp_layer(x, i, w1, b1, w2, b2, ln_g, ln_b):
    b, s, d = x.shape
    f = w1.shape[2]
    tm = TOKEN_TILE
    n = b * s
    tile = pl.BlockSpec((tm, d), lambda ti: (ti, 0))
    out = pl.pallas_call(
        _mlp_kernel,
        grid=(n // tm,),
        in_specs=[tile, _layer_block(i, (d, f)), _layer_block(i, (1, f)), _layer_block(i, (f, d)),
                  _layer_block(i, (1, d)), _layer_block(i, (1, d)), _layer_block(i, (1, d))],
        out_specs=tile,
        out_shape=jax.ShapeDtypeStruct((n, d), x.dtype),
        scratch_shapes=[pltpu.VMEM((tm, f), jnp.float32)],
        compiler_params=_params("arbitrary"),
        name="sqrelu_mlp",
    )(x.reshape(n, d), w1, _rows(b1), w2, _rows(b2), _rows(ln_g), _rows(ln_b))
    return out.reshape(b, s, d)


def _conv_layer(x, j, i, w_in, b_in, dw, dw_b, ln_g, ln_b, w_out, b_out, mix_g, mix_b):
    b, s, d = x.shape
    tm = TOKEN_TILE
    tile = pl.BlockSpec((1, tm, d), lambda bi, ti: (bi, ti, 0))
    vec_j = _layer_block(j, (1, d))
    vec_i = _layer_block(i, (1, d))
    return pl.pallas_call(
        _conv_kernel,
        grid=(b, s // tm),
        in_specs=[tile, _layer_block(j, (d, 2 * d)), _layer_block(j, (1, 2 * d)),
                  _layer_block(j, dw.shape[1:]), vec_j, vec_j, vec_j, _layer_block(j, (d, d)), vec_j,
                  vec_i, vec_i],
        out_specs=tile,
        out_shape=jax.ShapeDtypeStruct(x.shape, x.dtype),
        scratch_shapes=[pltpu.VMEM((d // LANES, CONV_HALO + tm, LANES), jnp.float32)],
        compiler_params=_params("arbitrary", "arbitrary"),
        name="conv_module",
    )(x, w_in, _rows(b_in), dw, _rows(dw_b), _rows(ln_g), _rows(ln_b), w_out, _rows(b_out),
      _rows(mix_g), _rows(mix_b))


def kernel(x, pool_w, pool_scale, conv_w_in, conv_b_in, conv_dw, conv_dw_b, conv_ln_g, conv_ln_b,
           conv_w_out, conv_b_out, mix_ln_g, mix_ln_b, mlp_w1, mlp_b1, mlp_w2, mlp_b2, mlp_ln_g, mlp_ln_b):
    for i in range(DEPTH):
        j = i // 2
        if i % 2 == 0:
            x = _pool_layer(x, j, i, pool_w, pool_scale, mix_ln_g, mix_ln_b)
        else:
            x = _conv_layer(x, j, i, conv_w_in, conv_b_in, conv_dw, conv_dw_b, conv_ln_g, conv_ln_b,
                            conv_w_out, conv_b_out, mix_ln_g, mix_ln_b)
        x = _mlp_layer(x, i, mlp_w1, mlp_b1, mlp_w2, mlp_b2, mlp_ln_g, mlp_ln_b)
    return x
```

```python
import functools

import jax
import jax.numpy as jnp
from jax.experimental import pallas as pl
from jax.experimental.pallas import tpu as pltpu

POOL_WINDOWS = (2, 4, 8, 16)
CONV_WIDTH = 31
DEPTH = 2
ALPHA = (2.0 * DEPTH) ** 0.25
LN_EPS = 1e-5

POOL_HALO = 16
CONV_HALO = 32
LANES = 128

TOKEN_TILE = 512
POOL_TOKEN_TILE = 1024
MLP_ROWS = 256
VMEM_LIMIT_BYTES = 56 * 1024 * 1024


def _layer_norm(z, g, b):
    mu = jnp.mean(z, axis=-1, keepdims=True)
    zc = z - mu
    var = jnp.mean(zc * zc, axis=-1, keepdims=True)
    return zc * jax.lax.rsqrt(var + LN_EPS) * g + b


def _dot(a, b):
    return jnp.dot(a, b, preferred_element_type=jnp.float32)


def _vec(ref, layer):
    return ref[layer:layer + 1, :]


def _pool_kernel(j, i_layer, x_ref, w_ref, scale_ref, g_ref, b_ref, o_ref, buf_ref):
    i = pl.program_id(1)
    tm = x_ref.shape[1]
    dg = w_ref.shape[1]
    n_slabs = buf_ref.shape[0]
    slabs_per_group = dg // LANES

    @pl.when(i == 0)
    def _():
        buf_ref[:, 0:POOL_HALO, :] = jnp.zeros((n_slabs, POOL_HALO, LANES), jnp.float32)

    x = x_ref[0]
    for l in range(n_slabs):
        buf_ref[l, POOL_HALO:, :] = x[:, l * LANES:(l + 1) * LANES]
    t = i * tm + jax.lax.broadcasted_iota(jnp.int32, (tm, 1), 0)
    tf = (t + 1).astype(jnp.float32)
    for g, w in enumerate(POOL_WINDOWS):
        cols = slice(g * dg, (g + 1) * dg)
        xg = x[:, cols]
        parts = []
        for l in range(g * slabs_per_group, (g + 1) * slabs_per_group):
            s = buf_ref[l, POOL_HALO:POOL_HALO + tm, :]
            for k in range(1, w):
                s = s + buf_ref[l, POOL_HALO - k:POOL_HALO - k + tm, :]
            parts.append(s)
        s = jnp.concatenate(parts, axis=-1)
        count = jnp.minimum(tf, float(w))
        d = s / count - xg
        z = ALPHA * xg + _dot(d, w_ref[g]) * scale_ref[j:j + 1, cols]
        o_ref[0, :, cols] = z
    buf_ref[:, 0:POOL_HALO, :] = buf_ref[:, tm:tm + POOL_HALO, :]
    o_ref[0] = _layer_norm(o_ref[0], _vec(g_ref, i_layer), _vec(b_ref, i_layer))


def _mlp_kernel(i_layer, x_ref, w1_ref, b1_ref, w2_ref, b2_ref, g_ref, b_ref, o_ref, h_ref):
    tm = x_ref.shape[0]
    for r0 in range(0, tm, MLP_ROWS):
        rows = slice(r0, r0 + MLP_ROWS)
        h = jnp.maximum(_dot(x_ref[rows, :], w1_ref[...]) + _vec(b1_ref, i_layer), 0.0)
        h_ref[rows, :] = h * h
    for r0 in range(0, tm, MLP_ROWS):
        rows = slice(r0, r0 + MLP_ROWS)
        y = _dot(h_ref[rows, :], w2_ref[...]) + _vec(b2_ref, i_layer)
        o_ref[rows, :] = _layer_norm(ALPHA * x_ref[rows, :] + y, _vec(g_ref, i_layer), _vec(b_ref, i_layer))


def _conv_kernel(j, i_layer, x_ref, win_ref, bin_ref, dw_ref, dwb_ref, lng_ref, lnb_ref, wout_ref,
                 bout_ref, g_ref, b_ref, o_ref, buf_ref):
    i = pl.program_id(1)
    tm = x_ref.shape[1]
    d = x_ref.shape[2]
    n_slabs = buf_ref.shape[0]

    @pl.when(i == 0)
    def _():
        buf_ref[:, 0:CONV_HALO, :] = jnp.zeros((n_slabs, CONV_HALO, LANES), jnp.float32)

    @pl.when(i >= 0)
    def _():
        h = _dot(x_ref[0], win_ref[...]) + _vec(bin_ref, j)
        glu = h[:, :d] * jax.nn.sigmoid(h[:, d:])
        for l in range(n_slabs):
            buf_ref[l, CONV_HALO:, :] = glu[:, l * LANES:(l + 1) * LANES]

    @pl.when(i >= -1)
    def _():
        base = CONV_HALO - (CONV_WIDTH - 1)
        for l in range(n_slabs):
            lanes = slice(l * LANES, (l + 1) * LANES)
            acc = buf_ref[l, base:base + tm, :] * dw_ref[0:1, lanes]
            for k in range(1, CONV_WIDTH):
                acc = acc + buf_ref[l, base + k:base + k + tm, :] * dw_ref[k:k + 1, lanes]
            o_ref[0, :, lanes] = acc + dwb_ref[j:j + 1, lanes]
        buf_ref[:, 0:CONV_HALO, :] = buf_ref[:, tm:tm + CONV_HALO, :]

    @pl.when(i >= -2)
    def _():
        c = _layer_norm(o_ref[0], _vec(lng_ref, j), _vec(lnb_ref, j))
        c = c * jax.nn.sigmoid(c)
        y = _dot(c, wout_ref[...]) + _vec(bout_ref, j)
        o_ref[0] = _layer_norm(ALPHA * x_ref[0] + y, _vec(g_ref, i_layer), _vec(b_ref, i_layer))


def _layer_block(layer, shape):
    nd = len(shape)
    return pl.BlockSpec((None,) + tuple(shape), lambda *_: (layer,) + (0,) * nd,
                        pipeline_mode=pl.Buffered(1))


def _whole(arr):
    nd = arr.ndim
    return pl.BlockSpec(arr.shape, lambda *_: (0,) * nd, pipeline_mode=pl.Buffered(1))


def _params(*dims):
    return pltpu.CompilerParams(dimension_semantics=dims, vmem_limit_bytes=VMEM_LIMIT_BYTES)


def _pool_layer(x, j, i, pool_w, pool_scale, ln_g, ln_b):
    b, s, d = x.shape
    tm = POOL_TOKEN_TILE
    tile = pl.BlockSpec((1, tm, d), lambda bi, ti: (bi, ti, 0))
    return pl.pallas_call(
        functools.partial(_pool_kernel, j, i),
        grid=(b, s // tm),
        in_specs=[tile, _layer_block(j, pool_w.shape[1:]), _whole(pool_scale), _whole(ln_g), _whole(ln_b)],
        out_specs=tile,
        out_shape=jax.ShapeDtypeStruct(x.shape, x.dtype),
        scratch_shapes=[pltpu.VMEM((d // LANES, POOL_HALO + tm, LANES), jnp.float32)],
        compiler_params=_params("arbitrary", "arbitrary"),
        name="pool_mixer",
    )(x, pool_w, pool_scale, ln_g, ln_b)


def _mlp_layer(x, i, w1, b1, w2, b2, ln_g, ln_b):
    b, s, d = x.shape
    f = w1.shape[2]
    tm = TOKEN_TILE
    n = b * s
    tile = pl.BlockSpec((tm, d), lambda ti: (ti, 0))
    out = pl.pallas_call(
        functools.partial(_mlp_kernel, i),
        grid=(n // tm,),
        in_specs=[tile, _layer_block(i, (d, f)), _whole(b1), _layer_block(i, (f, d)), _whole(b2),
                  _whole(ln_g), _whole(ln_b)],
        out_specs=tile,
        out_shape=jax.ShapeDtypeStruct((n, d), x.dtype),
        scratch_shapes=[pltpu.VMEM((tm, f), jnp.float32)],
        compiler_params=_params("arbitrary"),
        name="sqrelu_mlp",
    )(x.reshape(n, d), w1, b1, w2, b2, ln_g, ln_b)
    return out.reshape(b, s, d)


def _conv_layer(x, j, i, w_in, b_in, dw, dw_b, ln_g, ln_b, w_out, b_out, mix_g, mix_b):
    b, s, d = x.shape
    tm = TOKEN_TILE
    tile = pl.BlockSpec((1, tm, d), lambda bi, ti: (bi, ti, 0))
    return pl.pallas_call(
        functools.partial(_conv_kernel, j, i),
        grid=(b, s // tm),
        in_specs=[tile, _layer_block(j, (d, 2 * d)), _whole(b_in), _layer_block(j, dw.shape[1:]),
                  _whole(dw_b), _whole(ln_g), _whole(ln_b), _layer_block(j, (d, d)), _whole(b_out),
                  _whole(mix_g), _whole(mix_b)],
        out_specs=tile,
        out_shape=jax.ShapeDtypeStruct(x.shape, x.dtype),
        scratch_shapes=[pltpu.VMEM((d // LANES, CONV_HALO + tm, LANES), jnp.float32)],
        compiler_params=_params("arbitrary", "arbitrary"),
        name="conv_module",
    )(x, w_in, b_in, dw, dw_b, ln_g, ln_b, w_out, b_out, mix_g, mix_b)


def kernel(x, pool_w, pool_scale, conv_w_in, conv_b_in, conv_dw, conv_dw_b, conv_ln_g, conv_ln_b,
           conv_w_out, conv_b_out, mix_ln_g, mix_ln_b, mlp_w1, mlp_b1, mlp_w2, mlp_b2, mlp_ln_g, mlp_ln_b):
    for i in range(DEPTH):
        j = i // 2
        if i % 2 == 0:
            x = _pool_layer(x, j, i, pool_w, pool_scale, mix_ln_g, mix_ln_b)
        else:
            x = _conv_layer(x, j, i, conv_w_in, conv_b_in, conv_dw, conv_dw_b, conv_ln_g, conv_ln_b,
                            conv_w_out, conv_b_out, mix_ln_g, mix_ln_b)
        x = _mlp_layer(x, i, mlp_w1, mlp_b1, mlp_w2, mlp_b2, mlp_ln_g, mlp_ln_b)
    return x
```

```python
import functools

import jax
import jax.numpy as jnp
from jax.experimental import pallas as pl
from jax.experimental.pallas import tpu as pltpu

POOL_WINDOWS = (2, 4, 8, 16)
CONV_WIDTH = 31
DEPTH = 2
ALPHA = (2.0 * DEPTH) ** 0.25
LN_EPS = 1e-5

POOL_HALO = 16
CONV_HALO = 32
LANES = 128

TOKEN_TILE = 1024
MLP_ROWS = 256
MIB = 1024 * 1024
MIXER_VMEM_LIMIT_BYTES = 56 * MIB
MLP_VMEM_LIMIT_BYTES = 62 * MIB


def _layer_norm(z, g, b):
    mu = jnp.mean(z, axis=-1, keepdims=True)
    zc = z - mu
    var = jnp.mean(zc * zc, axis=-1, keepdims=True)
    return zc * jax.lax.rsqrt(var + LN_EPS) * g + b


def _dot(a, b):
    return jnp.dot(a, b, preferred_element_type=jnp.float32)


def _vec(ref, layer):
    return ref[layer:layer + 1, :]


def _pool_kernel(j, i_layer, x_ref, w_ref, scale_ref, g_ref, b_ref, o_ref, buf_ref):
    i = pl.program_id(1)
    tm = x_ref.shape[1]
    dg = w_ref.shape[1]
    n_slabs = buf_ref.shape[0]
    slabs_per_group = dg // LANES

    @pl.when(i == 0)
    def _():
        buf_ref[:, 0:POOL_HALO, :] = jnp.zeros((n_slabs, POOL_HALO, LANES), jnp.float32)

    x = x_ref[0]
    for l in range(n_slabs):
        buf_ref[l, POOL_HALO:, :] = x[:, l * LANES:(l + 1) * LANES]
    t = i * tm + jax.lax.broadcasted_iota(jnp.int32, (tm, 1), 0)
    tf = (t + 1).astype(jnp.float32)
    for g, w in enumerate(POOL_WINDOWS):
        cols = slice(g * dg, (g + 1) * dg)
        xg = x[:, cols]
        parts = []
        for l in range(g * slabs_per_group, (g + 1) * slabs_per_group):
            s = buf_ref[l, POOL_HALO:POOL_HALO + tm, :]
            for k in range(1, w):
                s = s + buf_ref[l, POOL_HALO - k:POOL_HALO - k + tm, :]
            parts.append(s)
        s = jnp.concatenate(parts, axis=-1)
        count = jnp.minimum(tf, float(w))
        d = s / count - xg
        z = ALPHA * xg + _dot(d, w_ref[g]) * scale_ref[j:j + 1, cols]
        o_ref[0, :, cols] = z
    buf_ref[:, 0:POOL_HALO, :] = buf_ref[:, tm:tm + POOL_HALO, :]
    o_ref[0] = _layer_norm(o_ref[0], _vec(g_ref, i_layer), _vec(b_ref, i_layer))


def _mlp_kernel(i_layer, x_ref, w1_ref, b1_ref, w2_ref, b2_ref, g_ref, b_ref, o_ref, h_ref):
    n_chunks = x_ref.shape[0] // MLP_ROWS

    def hidden(c):
        rows = slice(c * MLP_ROWS, (c + 1) * MLP_ROWS)
        h = jnp.maximum(_dot(x_ref[rows, :], w1_ref[...]) + _vec(b1_ref, i_layer), 0.0)
        h_ref[c % 2] = h * h

    def output(c):
        rows = slice(c * MLP_ROWS, (c + 1) * MLP_ROWS)
        y = _dot(h_ref[c % 2], w2_ref[...]) + _vec(b2_ref, i_layer)
        o_ref[rows, :] = _layer_norm(ALPHA * x_ref[rows, :] + y, _vec(g_ref, i_layer), _vec(b_ref, i_layer))

    hidden(0)
    for c in range(n_chunks):
        if c + 1 < n_chunks:
            hidden(c + 1)
        output(c)


def _conv_kernel(j, i_layer, x_ref, win_ref, bin_ref, dw_ref, dwb_ref, lng_ref, lnb_ref, wout_ref,
                 bout_ref, g_ref, b_ref, o_ref, buf_ref):
    i = pl.program_id(1)
    tm = x_ref.shape[1]
    d = x_ref.shape[2]
    n_slabs = buf_ref.shape[0]

    @pl.when(i == 0)
    def _():
        buf_ref[:, 0:CONV_HALO, :] = jnp.zeros((n_slabs, CONV_HALO, LANES), jnp.float32)

    @pl.when(i >= 0)
    def _():
        h = _dot(x_ref[0], win_ref[...]) + _vec(bin_ref, j)
        glu = h[:, :d] * jax.nn.sigmoid(h[:, d:])
        for l in range(n_slabs):
            buf_ref[l, CONV_HALO:, :] = glu[:, l * LANES:(l + 1) * LANES]

    @pl.when(i >= -1)
    def _():
        base = CONV_HALO - (CONV_WIDTH - 1)
        for l in range(n_slabs):
            lanes = slice(l * LANES, (l + 1) * LANES)
            acc = buf_ref[l, base:base + tm, :] * dw_ref[0:1, lanes]
            for k in range(1, CONV_WIDTH):
                acc = acc + buf_ref[l, base + k:base + k + tm, :] * dw_ref[k:k + 1, lanes]
            o_ref[0, :, lanes] = acc + dwb_ref[j:j + 1, lanes]
        buf_ref[:, 0:CONV_HALO, :] = buf_ref[:, tm:tm + CONV_HALO, :]

    @pl.when(i >= -2)
    def _():
        c = _layer_norm(o_ref[0], _vec(lng_ref, j), _vec(lnb_ref, j))
        c = c * jax.nn.sigmoid(c)
        y = _dot(c, wout_ref[...]) + _vec(bout_ref, j)
        o_ref[0] = _layer_norm(ALPHA * x_ref[0] + y, _vec(g_ref, i_layer), _vec(b_ref, i_layer))


def _layer_block(layer, shape):
    nd = len(shape)
    return pl.BlockSpec((None,) + tuple(shape), lambda *_: (layer,) + (0,) * nd,
                        pipeline_mode=pl.Buffered(1))


def _whole(arr):
    nd = arr.ndim
    return pl.BlockSpec(arr.shape, lambda *_: (0,) * nd, pipeline_mode=pl.Buffered(1))


def _params(vmem_limit_bytes, *dims):
    return pltpu.CompilerParams(dimension_semantics=dims, vmem_limit_bytes=vmem_limit_bytes)


def _pool_layer(x, j, i, pool_w, pool_scale, ln_g, ln_b):
    b, s, d = x.shape
    tm = TOKEN_TILE
    tile = pl.BlockSpec((1, tm, d), lambda bi, ti: (bi, ti, 0))
    return pl.pallas_call(
        functools.partial(_pool_kernel, j, i),
        grid=(b, s // tm),
        in_specs=[tile, _layer_block(j, pool_w.shape[1:]), _whole(pool_scale), _whole(ln_g), _whole(ln_b)],
        out_specs=tile,
        out_shape=jax.ShapeDtypeStruct(x.shape, x.dtype),
        scratch_shapes=[pltpu.VMEM((d // LANES, POOL_HALO + tm, LANES), jnp.float32)],
        compiler_params=_params(MIXER_VMEM_LIMIT_BYTES, "arbitrary", "arbitrary"),
        name="pool_mixer",
    )(x, pool_w, pool_scale, ln_g, ln_b)


def _mlp_layer(x, i, w1, b1, w2, b2, ln_g, ln_b):
    b, s, d = x.shape
    f = w1.shape[2]
    tm = TOKEN_TILE
    n = b * s
    tile = pl.BlockSpec((tm, d), lambda ti: (ti, 0))
    out = pl.pallas_call(
        functools.partial(_mlp_kernel, i),
        grid=(n // tm,),
        in_specs=[tile, _layer_block(i, (d, f)), _whole(b1), _layer_block(i, (f, d)), _whole(b2),
                  _whole(ln_g), _whole(ln_b)],
        out_specs=tile,
        out_shape=jax.ShapeDtypeStruct((n, d), x.dtype),
        scratch_shapes=[pltpu.VMEM((2, MLP_ROWS, f), jnp.float32)],
        compiler_params=_params(MLP_VMEM_LIMIT_BYTES, "arbitrary"),
        name="sqrelu_mlp",
    )(x.reshape(n, d), w1, b1, w2, b2, ln_g, ln_b)
    return out.reshape(b, s, d)


def _conv_layer(x, j, i, w_in, b_in, dw, dw_b, ln_g, ln_b, w_out, b_out, mix_g, mix_b):
    b, s, d = x.shape
    tm = TOKEN_TILE
    tile = pl.BlockSpec((1, tm, d), lambda bi, ti: (bi, ti, 0))
    return pl.pallas_call(
        functools.partial(_conv_kernel, j, i),
        grid=(b, s // tm),
        in_specs=[tile, _layer_block(j, (d, 2 * d)), _whole(b_in), _layer_block(j, dw.shape[1:]),
                  _whole(dw_b), _whole(ln_g), _whole(ln_b), _layer_block(j, (d, d)), _whole(b_out),
                  _whole(mix_g), _whole(mix_b)],
        out_specs=tile,
        out_shape=jax.ShapeDtypeStruct(x.shape, x.dtype),
        scratch_shapes=[pltpu.VMEM((d // LANES, CONV_HALO + tm, LANES), jnp.float32)],
        compiler_params=_params(MIXER_VMEM_LIMIT_BYTES, "arbitrary", "arbitrary"),
        name="conv_module",
    )(x, w_in, b_in, dw, dw_b, ln_g, ln_b, w_out, b_out, mix_g, mix_b)


def kernel(x, pool_w, pool_scale, conv_w_in, conv_b_in, conv_dw, conv_dw_b, conv_ln_g, conv_ln_b,
           conv_w_out, conv_b_out, mix_ln_g, mix_ln_b, mlp_w1, mlp_b1, mlp_w2, mlp_b2, mlp_ln_g, mlp_ln_b):
    for i in range(DEPTH):
        j = i // 2
        if i % 2 == 0:
            x = _pool_layer(x, j, i, pool_w, pool_scale, mix_ln_g, mix_ln_b)
        else:
            x = _conv_layer(x, j, i, conv_w_in, conv_b_in, conv_dw, conv_dw_b, conv_ln_g, conv_ln_b,
                            conv_w_out, conv_b_out, mix_ln_g, mix_ln_b)
        x = _mlp_layer(x, i, mlp_w1, mlp_b1, mlp_w2, mlp_b2, mlp_ln_g, mlp_ln_b)
    return x
```

```python
import functools

import jax
import jax.numpy as jnp
from jax.experimental import pallas as pl
from jax.experimental.pallas import tpu as pltpu

POOL_WINDOWS = (2, 4, 8, 16)
CONV_WIDTH = 31
DEPTH = 2
ALPHA = (2.0 * DEPTH) ** 0.25
LN_EPS = 1e-5

SUBLANES = 8
LANES = 128
POOL_HALO = 32
CONV_HALO = 32

TOKEN_TILE = 512
POOL_TOKEN_TILE = 1024
MLP_ROWS = 256
VMEM_LIMIT_BYTES = 56 * 1024 * 1024


def _layer_norm(z, g, b):
    mu = jnp.mean(z, axis=-1, keepdims=True)
    zc = z - mu
    var = jnp.mean(zc * zc, axis=-1, keepdims=True)
    return zc * jax.lax.rsqrt(var + LN_EPS) * g + b


def _dot(a, b):
    return jnp.dot(a, b, preferred_element_type=jnp.float32)


def _vec(ref, layer):
    return ref[layer:layer + 1, :]


def _window_sum(buf_ref, tmp_ref, l, w, tm):
    levels = w.bit_length() - 1
    assert w == 1 << levels and SUBLANES * (levels - 1) + 1 <= POOL_HALO
    src = buf_ref.at[l]
    end = POOL_HALO + tm
    for k in range(1, levels + 1):
        shift = 1 << (k - 1)
        start = POOL_HALO - SUBLANES * (levels - k)
        cur = src[start:end, :] + src[start - shift:end - shift, :]
        if k == levels:
            return cur
        tmp_ref[k % 2, start:end, :] = cur
        src = tmp_ref.at[k % 2]


def _pool_kernel(j, i_layer, x_ref, w_ref, scale_ref, g_ref, b_ref, o_ref, buf_ref, tmp_ref):
    i = pl.program_id(1)
    tm = x_ref.shape[1]
    dg = w_ref.shape[1]
    n_slabs = buf_ref.shape[0]
    slabs_per_group = dg // LANES

    @pl.when(i == 0)
    def _():
        buf_ref[:, 0:POOL_HALO, :] = jnp.zeros((n_slabs, POOL_HALO, LANES), jnp.float32)

    x = x_ref[0]
    for l in range(n_slabs):
        buf_ref[l, POOL_HALO:, :] = x[:, l * LANES:(l + 1) * LANES]
    t = i * tm + jax.lax.broadcasted_iota(jnp.int32, (tm, 1), 0)
    tf = (t + 1).astype(jnp.float32)
    for g, w in enumerate(POOL_WINDOWS):
        cols = slice(g * dg, (g + 1) * dg)
        xg = x[:, cols]
        parts = [_window_sum(buf_ref, tmp_ref, l, w, tm)
                 for l in range(g * slabs_per_group, (g + 1) * slabs_per_group)]
        s = jnp.concatenate(parts, axis=-1)
        count = jnp.minimum(tf, float(w))
        d = s / count - xg
        z = ALPHA * xg + _dot(d, w_ref[g]) * scale_ref[j:j + 1, cols]
        o_ref[0, :, cols] = z
    buf_ref[:, 0:POOL_HALO, :] = buf_ref[:, tm:tm + POOL_HALO, :]
    o_ref[0] = _layer_norm(o_ref[0], _vec(g_ref, i_layer), _vec(b_ref, i_layer))


def _mlp_kernel(i_layer, x_ref, w1_ref, b1_ref, w2_ref, b2_ref, g_ref, b_ref, o_ref, h_ref):
    tm = x_ref.shape[0]
    for r0 in range(0, tm, MLP_ROWS):
        rows = slice(r0, r0 + MLP_ROWS)
        h = jnp.maximum(_dot(x_ref[rows, :], w1_ref[...]) + _vec(b1_ref, i_layer), 0.0)
        h_ref[rows, :] = h * h
    for r0 in range(0, tm, MLP_ROWS):
        rows = slice(r0, r0 + MLP_ROWS)
        y = _dot(h_ref[rows, :], w2_ref[...]) + _vec(b2_ref, i_layer)
        o_ref[rows, :] = _layer_norm(ALPHA * x_ref[rows, :] + y, _vec(g_ref, i_layer), _vec(b_ref, i_layer))


def _conv_kernel(j, i_layer, x_ref, win_ref, bin_ref, dw_ref, dwb_ref, lng_ref, lnb_ref, wout_ref,
                 bout_ref, g_ref, b_ref, o_ref, buf_ref):
    i = pl.program_id(1)
    tm = x_ref.shape[1]
    d = x_ref.shape[2]
    n_slabs = buf_ref.shape[0]

    @pl.when(i == 0)
    def _():
        buf_ref[:, 0:CONV_HALO, :] = jnp.zeros((n_slabs, CONV_HALO, LANES), jnp.float32)

    @pl.when(i >= 0)
    def _():
        h = _dot(x_ref[0], win_ref[...]) + _vec(bin_ref, j)
        glu = h[:, :d] * jax.nn.sigmoid(h[:, d:])
        for l in range(n_slabs):
            buf_ref[l, CONV_HALO:, :] = glu[:, l * LANES:(l + 1) * LANES]

    @pl.when(i >= -1)
    def _():
        base = CONV_HALO - (CONV_WIDTH - 1)
        for l in range(n_slabs):
            lanes = slice(l * LANES, (l + 1) * LANES)
            acc = buf_ref[l, base:base + tm, :] * dw_ref[0:1, lanes]
            for k in range(1, CONV_WIDTH):
                acc = acc + buf_ref[l, base + k:base + k + tm, :] * dw_ref[k:k + 1, lanes]
            o_ref[0, :, lanes] = acc + dwb_ref[j:j + 1, lanes]
        buf_ref[:, 0:CONV_HALO, :] = buf_ref[:, tm:tm + CONV_HALO, :]

    @pl.when(i >= -2)
    def _():
        c = _layer_norm(o_ref[0], _vec(lng_ref, j), _vec(lnb_ref, j))
        c = c * jax.nn.sigmoid(c)
        y = _dot(c, wout_ref[...]) + _vec(bout_ref, j)
        o_ref[0] = _layer_norm(ALPHA * x_ref[0] + y, _vec(g_ref, i_layer), _vec(b_ref, i_layer))


def _layer_block(layer, shape):
    nd = len(shape)
    return pl.BlockSpec((None,) + tuple(shape), lambda *_: (layer,) + (0,) * nd,
                        pipeline_mode=pl.Buffered(1))


def _whole(arr):
    nd = arr.ndim
    return pl.BlockSpec(arr.shape, lambda *_: (0,) * nd, pipeline_mode=pl.Buffered(1))


def _params(*dims):
    return pltpu.CompilerParams(dimension_semantics=dims, vmem_limit_bytes=VMEM_LIMIT_BYTES)


def _pool_layer(x, j, i, pool_w, pool_scale, ln_g, ln_b):
    b, s, d = x.shape
    tm = POOL_TOKEN_TILE
    tile = pl.BlockSpec((1, tm, d), lambda bi, ti: (bi, ti, 0))
    return pl.pallas_call(
        functools.partial(_pool_kernel, j, i),
        grid=(b, s // tm),
        in_specs=[tile, _layer_block(j, pool_w.shape[1:]), _whole(pool_scale), _whole(ln_g), _whole(ln_b)],
        out_specs=tile,
        out_shape=jax.ShapeDtypeStruct(x.shape, x.dtype),
        scratch_shapes=[pltpu.VMEM((d // LANES, POOL_HALO + tm, LANES), jnp.float32),
                        pltpu.VMEM((2, POOL_HALO + tm, LANES), jnp.float32)],
        compiler_params=_params("arbitrary", "arbitrary"),
        name="pool_mixer",
    )(x, pool_w, pool_scale, ln_g, ln_b)


def _mlp_layer(x, i, w1, b1, w2, b2, ln_g, ln_b):
    b, s, d = x.shape
    f = w1.shape[2]
    tm = TOKEN_TILE
    n = b * s
    tile = pl.BlockSpec((tm, d), lambda ti: (ti, 0))
    out = pl.pallas_call(
        functools.partial(_mlp_kernel, i),
        grid=(n // tm,),
        in_specs=[tile, _layer_block(i, (d, f)), _whole(b1), _layer_block(i, (f, d)), _whole(b2),
                  _whole(ln_g), _whole(ln_b)],
        out_specs=tile,
        out_shape=jax.ShapeDtypeStruct((n, d), x.dtype),
        scratch_shapes=[pltpu.VMEM((tm, f), jnp.float32)],
        compiler_params=_params("arbitrary"),
        name="sqrelu_mlp",
    )(x.reshape(n, d), w1, b1, w2, b2, ln_g, ln_b)
    return out.reshape(b, s, d)


def _conv_layer(x, j, i, w_in, b_in, dw, dw_b, ln_g, ln_b, w_out, b_out, mix_g, mix_b):
    b, s, d = x.shape
    tm = TOKEN_TILE
    tile = pl.BlockSpec((1, tm, d), lambda bi, ti: (bi, ti, 0))
    return pl.pallas_call(
        functools.partial(_conv_kernel, j, i),
        grid=(b, s // tm),
        in_specs=[tile, _layer_block(j, (d, 2 * d)), _whole(b_in), _layer_block(j, dw.shape[1:]),
                  _whole(dw_b), _whole(ln_g), _whole(ln_b), _layer_block(j, (d, d)), _whole(b_out),
                  _whole(mix_g), _whole(mix_b)],
        out_specs=tile,
        out_shape=jax.ShapeDtypeStruct(x.shape, x.dtype),
        scratch_shapes=[pltpu.VMEM((d // LANES, CONV_HALO + tm, LANES), jnp.float32)],
        compiler_params=_params("arbitrary", "arbitrary"),
        name="conv_module",
    )(x, w_in, b_in, dw, dw_b, ln_g, ln_b, w_out, b_out, mix_g, mix_b)


def kernel(x, pool_w, pool_scale, conv_w_in, conv_b_in, conv_dw, conv_dw_b, conv_ln_g, conv_ln_b,
           conv_w_out, conv_b_out, mix_ln_g, mix_ln_b, mlp_w1, mlp_b1, mlp_w2, mlp_b2, mlp_ln_g, mlp_ln_b):
    for i in range(DEPTH):
        j = i // 2
        if i % 2 == 0:
            x = _pool_layer(x, j, i, pool_w, pool_scale, mix_ln_g, mix_ln_b)
        else:
            x = _conv_layer(x, j, i, conv_w_in, conv_b_in, conv_dw, conv_dw_b, conv_ln_g, conv_ln_b,
                            conv_w_out, conv_b_out, mix_ln_g, mix_ln_b)
        x = _mlp_layer(x, i, mlp_w1, mlp_b1, mlp_w2, mlp_b2, mlp_ln_g, mlp_ln_b)
    return x
```

```python
import functools

import jax
import jax.numpy as jnp
from jax.experimental import pallas as pl
from jax.experimental.pallas import tpu as pltpu

POOL_WINDOWS = (2, 4, 8, 16)
CONV_WIDTH = 31
DEPTH = 2
ALPHA = (2.0 * DEPTH) ** 0.25
LN_EPS = 1e-5

SUBLANES = 8
LANES = 128
POOL_HALO = 32
CONV_HALO = 32

TOKEN_TILE = 512
MLP_ROWS = 256
VMEM_LIMIT_BYTES = 56 * 1024 * 1024


def _layer_norm(z, g, b):
    mu = jnp.mean(z, axis=-1, keepdims=True)
    zc = z - mu
    var = jnp.mean(zc * zc, axis=-1, keepdims=True)
    return zc * jax.lax.rsqrt(var + LN_EPS) * g + b


def _dot(a, b):
    return jnp.dot(a, b, preferred_element_type=jnp.float32)


def _vec(ref, layer):
    return ref[layer:layer + 1, :]


def _window_sum(buf_ref, tmp_ref, l, w, tm):
    levels = w.bit_length() - 1
    assert w == 1 << levels and SUBLANES * (levels - 1) + 1 <= POOL_HALO
    src = buf_ref.at[l]
    end = POOL_HALO + tm
    for k in range(1, levels + 1):
        shift = 1 << (k - 1)
        start = POOL_HALO - SUBLANES * (levels - k)
        cur = src[start:end, :] + src[start - shift:end - shift, :]
        if k == levels:
            return cur
        tmp_ref[k % 2, start:end, :] = cur
        src = tmp_ref.at[k % 2]


def _pool_tile(j, i_layer, x_ref, w_ref, scale_ref, g_ref, b_ref, dst_ref, buf_ref, tmp_ref):
    i = pl.program_id(1)
    tm = x_ref.shape[1]
    dg = w_ref.shape[1]
    n_slabs = buf_ref.shape[0]
    slabs_per_group = dg // LANES

    @pl.when(i == 0)
    def _():
        buf_ref[:, 0:POOL_HALO, :] = jnp.zeros((n_slabs, POOL_HALO, LANES), jnp.float32)

    x = x_ref[0]
    for l in range(n_slabs):
        buf_ref[l, POOL_HALO:, :] = x[:, l * LANES:(l + 1) * LANES]
    t = i * tm + jax.lax.broadcasted_iota(jnp.int32, (tm, 1), 0)
    tf = (t + 1).astype(jnp.float32)
    for g, w in enumerate(POOL_WINDOWS):
        cols = slice(g * dg, (g + 1) * dg)
        xg = x[:, cols]
        parts = [_window_sum(buf_ref, tmp_ref, l, w, tm)
                 for l in range(g * slabs_per_group, (g + 1) * slabs_per_group)]
        s = jnp.concatenate(parts, axis=-1)
        count = jnp.minimum(tf, float(w))
        d = s / count - xg
        z = ALPHA * xg + _dot(d, w_ref[g]) * scale_ref[j:j + 1, cols]
        dst_ref[:, cols] = z
    buf_ref[:, 0:POOL_HALO, :] = buf_ref[:, tm:tm + POOL_HALO, :]
    dst_ref[...] = _layer_norm(dst_ref[...], _vec(g_ref, i_layer), _vec(b_ref, i_layer))


def _mlp_tile(i_layer, x_ref, w1_ref, b1_ref, w2_ref, b2_ref, g_ref, b_ref, o_ref, h_ref):
    tm = x_ref.shape[0]
    for r0 in range(0, tm, MLP_ROWS):
        rows = slice(r0, r0 + MLP_ROWS)
        h = jnp.maximum(_dot(x_ref[rows, :], w1_ref[...]) + _vec(b1_ref, i_layer), 0.0)
        h_ref[rows, :] = h * h
    for r0 in range(0, tm, MLP_ROWS):
        rows = slice(r0, r0 + MLP_ROWS)
        y = _dot(h_ref[rows, :], w2_ref[...]) + _vec(b2_ref, i_layer)
        o_ref[rows, :] = _layer_norm(ALPHA * x_ref[rows, :] + y, _vec(g_ref, i_layer), _vec(b_ref, i_layer))


def _mlp_kernel(i_layer, x_ref, w1_ref, b1_ref, w2_ref, b2_ref, g_ref, b_ref, o_ref, h_ref):
    _mlp_tile(i_layer, x_ref, w1_ref, b1_ref, w2_ref, b2_ref, g_ref, b_ref, o_ref, h_ref)


def _pool_mlp_kernel(j, i_layer, x_ref, pw_ref, scale_ref, mg_ref, mb_ref, w1_ref, b1_ref, w2_ref, b2_ref,
                     g_ref, b_ref, o_ref, buf_ref, tmp_ref, mix_ref, h_ref):
    i = pl.program_id(1)

    @pl.when(i >= 0)
    def _():
        _pool_tile(j, i_layer, x_ref, pw_ref, scale_ref, mg_ref, mb_ref, mix_ref, buf_ref, tmp_ref)

    @pl.when(i >= -1)
    def _():
        _mlp_tile(i_layer, mix_ref, w1_ref, b1_ref, w2_ref, b2_ref, g_ref, b_ref, o_ref.at[0], h_ref)


def _conv_kernel(j, i_layer, x_ref, win_ref, bin_ref, dw_ref, dwb_ref, lng_ref, lnb_ref, wout_ref,
                 bout_ref, g_ref, b_ref, o_ref, buf_ref):
    i = pl.program_id(1)
    tm = x_ref.shape[1]
    d = x_ref.shape[2]
    n_slabs = buf_ref.shape[0]

    @pl.when(i == 0)
    def _():
        buf_ref[:, 0:CONV_HALO, :] = jnp.zeros((n_slabs, CONV_HALO, LANES), jnp.float32)

    @pl.when(i >= 0)
    def _():
        h = _dot(x_ref[0], win_ref[...]) + _vec(bin_ref, j)
        glu = h[:, :d] * jax.nn.sigmoid(h[:, d:])
        for l in range(n_slabs):
            buf_ref[l, CONV_HALO:, :] = glu[:, l * LANES:(l + 1) * LANES]

    @pl.when(i >= -1)
    def _():
        base = CONV_HALO - (CONV_WIDTH - 1)
        for l in range(n_slabs):
            lanes = slice(l * LANES, (l + 1) * LANES)
            acc = buf_ref[l, base:base + tm, :] * dw_ref[0:1, lanes]
            for k in range(1, CONV_WIDTH):
                acc = acc + buf_ref[l, base + k:base + k + tm, :] * dw_ref[k:k + 1, lanes]
            o_ref[0, :, lanes] = acc + dwb_ref[j:j + 1, lanes]
        buf_ref[:, 0:CONV_HALO, :] = buf_ref[:, tm:tm + CONV_HALO, :]

    @pl.when(i >= -2)
    def _():
        c = _layer_norm(o_ref[0], _vec(lng_ref, j), _vec(lnb_ref, j))
        c = c * jax.nn.sigmoid(c)
        y = _dot(c, wout_ref[...]) + _vec(bout_ref, j)
        o_ref[0] = _layer_norm(ALPHA * x_ref[0] + y, _vec(g_ref, i_layer), _vec(b_ref, i_layer))


def _layer_block(layer, shape):
    nd = len(shape)
    return pl.BlockSpec((None,) + tuple(shape), lambda *_: (layer,) + (0,) * nd,
                        pipeline_mode=pl.Buffered(1))


def _whole(arr):
    nd = arr.ndim
    return pl.BlockSpec(arr.shape, lambda *_: (0,) * nd, pipeline_mode=pl.Buffered(1))


def _params(*dims):
    return pltpu.CompilerParams(dimension_semantics=dims, vmem_limit_bytes=VMEM_LIMIT_BYTES)


def _pool_mlp_layer(x, j, i, pool_w, pool_scale, mix_g, mix_b, w1, b1, w2, b2, ln_g, ln_b):
    b, s, d = x.shape
    f = w1.shape[2]
    tm = TOKEN_TILE
    tile = pl.BlockSpec((1, tm, d), lambda bi, ti: (bi, ti, 0))
    return pl.pallas_call(
        functools.partial(_pool_mlp_kernel, j, i),
        grid=(b, s // tm),
        in_specs=[tile, _layer_block(j, pool_w.shape[1:]), _whole(pool_scale), _whole(mix_g), _whole(mix_b),
                  _layer_block(i, (d, f)), _whole(b1), _layer_block(i, (f, d)), _whole(b2),
                  _whole(ln_g), _whole(ln_b)],
        out_specs=tile,
        out_shape=jax.ShapeDtypeStruct(x.shape, x.dtype),
        scratch_shapes=[pltpu.VMEM((d // LANES, POOL_HALO + tm, LANES), jnp.float32),
                        pltpu.VMEM((2, POOL_HALO + tm, LANES), jnp.float32),
                        pltpu.VMEM((tm, d), jnp.float32),
                        pltpu.VMEM((tm, f), jnp.float32)],
        compiler_params=_params("arbitrary", "arbitrary"),
        name="pool_mlp",
    )(x, pool_w, pool_scale, mix_g, mix_b, w1, b1, w2, b2, ln_g, ln_b)


def _mlp_layer(x, i, w1, b1, w2, b2, ln_g, ln_b):
    b, s, d = x.shape
    f = w1.shape[2]
    tm = TOKEN_TILE
    n = b * s
    tile = pl.BlockSpec((tm, d), lambda ti: (ti, 0))
    out = pl.pallas_call(
        functools.partial(_mlp_kernel, i),
        grid=(n // tm,),
        in_specs=[tile, _layer_block(i, (d, f)), _whole(b1), _layer_block(i, (f, d)), _whole(b2),
                  _whole(ln_g), _whole(ln_b)],
        out_specs=tile,
        out_shape=jax.ShapeDtypeStruct((n, d), x.dtype),
        scratch_shapes=[pltpu.VMEM((tm, f), jnp.float32)],
        compiler_params=_params("arbitrary"),
        name="sqrelu_mlp",
    )(x.reshape(n, d), w1, b1, w2, b2, ln_g, ln_b)
    return out.reshape(b, s, d)


def _conv_layer(x, j, i, w_in, b_in, dw, dw_b, ln_g, ln_b, w_out, b_out, mix_g, mix_b):
    b, s, d = x.shape
    tm = TOKEN_TILE
    tile = pl.BlockSpec((1, tm, d), lambda bi, ti: (bi, ti, 0))
    return pl.pallas_call(
        functools.partial(_conv_kernel, j, i),
        grid=(b, s // tm),
        in_specs=[tile, _layer_block(j, (d, 2 * d)), _whole(b_in), _layer_block(j, dw.shape[1:]),
                  _whole(dw_b), _whole(ln_g), _whole(ln_b), _layer_block(j, (d, d)), _whole(b_out),
                  _whole(mix_g), _whole(mix_b)],
        out_specs=tile,
        out_shape=jax.ShapeDtypeStruct(x.shape, x.dtype),
        scratch_shapes=[pltpu.VMEM((d // LANES, CONV_HALO + tm, LANES), jnp.float32)],
        compiler_params=_params("arbitrary", "arbitrary"),
        name="conv_module",
    )(x, w_in, b_in, dw, dw_b, ln_g, ln_b, w_out, b_out, mix_g, mix_b)


def kernel(x, pool_w, pool_scale, conv_w_in, conv_b_in, conv_dw, conv_dw_b, conv_ln_g, conv_ln_b,
           conv_w_out, conv_b_out, mix_ln_g, mix_ln_b, mlp_w1, mlp_b1, mlp_w2, mlp_b2, mlp_ln_g, mlp_ln_b):
    for i in range(DEPTH):
        j = i // 2
        if i % 2 == 0:
            x = _pool_mlp_layer(x, j, i, pool_w, pool_scale, mix_ln_g, mix_ln_b,
                                mlp_w1, mlp_b1, mlp_w2, mlp_b2, mlp_ln_g, mlp_ln_b)
        else:
            x = _conv_layer(x, j, i, conv_w_in, conv_b_in, conv_dw, conv_dw_b, conv_ln_g, conv_ln_b,
                            conv_w_out, conv_b_out, mix_ln_g, mix_ln_b)
            x = _mlp_layer(x, i, mlp_w1, mlp_b1, mlp_w2, mlp_b2, mlp_ln_g, mlp_ln_b)
    return x
```

```python
import functools

import jax
import jax.numpy as jnp
from jax.experimental import pallas as pl
from jax.experimental.pallas import tpu as pltpu

POOL_WINDOWS = (2, 4, 8, 16)
CONV_WIDTH = 31
DEPTH = 2
ALPHA = (2.0 * DEPTH) ** 0.25
LN_EPS = 1e-5

SUBLANES = 8
LANES = 128
POOL_HALO = 32
CONV_HALO = 32

TOKEN_TILE = 512
BF16_MLP_TOKEN_TILE = 1024
MLP_ROWS = 256
GLU_ROWS = 256
VMEM_LIMIT_BYTES = 56 * 1024 * 1024


def _layer_norm(z, g, b):
    mu = jnp.mean(z, axis=-1, keepdims=True)
    zc = z - mu
    var = jnp.mean(zc * zc, axis=-1, keepdims=True)
    return zc * jax.lax.rsqrt(var + LN_EPS) * g + b


def _dot(a, b):
    return jnp.dot(a, b, preferred_element_type=jnp.float32)


def _vec(ref, layer):
    return ref[layer:layer + 1, :]


def _window_sum(buf_ref, tmp_ref, l, w, tm):
    levels = w.bit_length() - 1
    assert w == 1 << levels and SUBLANES * (levels - 1) + 1 <= POOL_HALO
    src = buf_ref.at[l]
    end = POOL_HALO + tm
    for k in range(1, levels + 1):
        shift = 1 << (k - 1)
        start = POOL_HALO - SUBLANES * (levels - k)
        cur = src[start:end, :] + src[start - shift:end - shift, :]
        if k == levels:
            return cur
        tmp_ref[k % 2, start:end, :] = cur
        src = tmp_ref.at[k % 2]


def _pool_tile(j, i_layer, x_ref, w_ref, scale_ref, g_ref, b_ref, dst_ref, buf_ref, tmp_ref):
    i = pl.program_id(1)
    tm = x_ref.shape[1]
    dg = w_ref.shape[1]
    n_slabs = buf_ref.shape[0]
    slabs_per_group = dg // LANES

    @pl.when(i == 0)
    def _():
        buf_ref[:, 0:POOL_HALO, :] = jnp.zeros((n_slabs, POOL_HALO, LANES), jnp.float32)

    x = x_ref[0]
    for l in range(n_slabs):
        buf_ref[l, POOL_HALO:, :] = x[:, l * LANES:(l + 1) * LANES]
    t = i * tm + jax.lax.broadcasted_iota(jnp.int32, (tm, 1), 0)
    tf = (t + 1).astype(jnp.float32)
    for g, w in enumerate(POOL_WINDOWS):
        cols = slice(g * dg, (g + 1) * dg)
        xg = x[:, cols]
        parts = [_window_sum(buf_ref, tmp_ref, l, w, tm)
                 for l in range(g * slabs_per_group, (g + 1) * slabs_per_group)]
        s = jnp.concatenate(parts, axis=-1)
        count = jnp.minimum(tf, float(w))
        d = s / count - xg
        z = ALPHA * xg + _dot(d, w_ref[g]) * scale_ref[j:j + 1, cols]
        dst_ref[:, cols] = z
    buf_ref[:, 0:POOL_HALO, :] = buf_ref[:, tm:tm + POOL_HALO, :]
    dst_ref[...] = _layer_norm(dst_ref[...], _vec(g_ref, i_layer), _vec(b_ref, i_layer))


def _mlp_tile(i_layer, x_ref, w1_ref, b1_ref, w2_ref, b2_ref, g_ref, b_ref, o_ref, h_ref):
    tm = x_ref.shape[0]
    for r0 in range(0, tm, MLP_ROWS):
        rows = slice(r0, r0 + MLP_ROWS)
        xw = x_ref[rows, :].astype(w1_ref.dtype)
        h = jnp.maximum(_dot(xw, w1_ref[...]) + _vec(b1_ref, i_layer), 0.0)
        h_ref[rows, :] = (h * h).astype(h_ref.dtype)
    for r0 in range(0, tm, MLP_ROWS):
        rows = slice(r0, r0 + MLP_ROWS)
        y = _dot(h_ref[rows, :], w2_ref[...]) + _vec(b2_ref, i_layer)
        o_ref[rows, :] = _layer_norm(ALPHA * x_ref[rows, :] + y, _vec(g_ref, i_layer), _vec(b_ref, i_layer))


def _mlp_kernel(i_layer, x_ref, w1_ref, b1_ref, w2_ref, b2_ref, g_ref, b_ref, o_ref, h_ref):
    _mlp_tile(i_layer, x_ref, w1_ref, b1_ref, w2_ref, b2_ref, g_ref, b_ref, o_ref, h_ref)


def _pool_mlp_kernel(j, i_layer, x_ref, pw_ref, scale_ref, mg_ref, mb_ref, w1_ref, b1_ref, w2_ref, b2_ref,
                     g_ref, b_ref, o_ref, buf_ref, tmp_ref, mix_ref, h_ref):
    i = pl.program_id(1)

    @pl.when(i >= 0)
    def _():
        _pool_tile(j, i_layer, x_ref, pw_ref, scale_ref, mg_ref, mb_ref, mix_ref, buf_ref, tmp_ref)

    @pl.when(i >= -1)
    def _():
        _mlp_tile(i_layer, mix_ref, w1_ref, b1_ref, w2_ref, b2_ref, g_ref, b_ref, o_ref.at[0], h_ref)


def _conv_kernel(j, i_layer, x_ref, win_ref, bin_ref, dw_ref, dwb_ref, lng_ref, lnb_ref, wout_ref,
                 bout_ref, g_ref, b_ref, w1_ref, w2_ref, o_ref, w1c_ref, w2c_ref, buf_ref):
    i = pl.program_id(1)
    tm = x_ref.shape[1]
    d = x_ref.shape[2]
    n_slabs = buf_ref.shape[0]

    @pl.when(i == 0)
    def _():
        buf_ref[:, 0:CONV_HALO, :] = jnp.zeros((n_slabs, CONV_HALO, LANES), jnp.float32)

    @pl.when(i >= 0)
    def _():
        for r0 in range(0, tm, GLU_ROWS):
            h = _dot(x_ref[0, r0:r0 + GLU_ROWS, :], win_ref[...]) + _vec(bin_ref, j)
            glu = h[:, :d] * jax.nn.sigmoid(h[:, d:])
            for l in range(n_slabs):
                buf_ref[l, CONV_HALO + r0:CONV_HALO + r0 + GLU_ROWS, :] = glu[:, l * LANES:(l + 1) * LANES]
        w1c_ref[...] = w1_ref[...].astype(w1c_ref.dtype)
        w2c_ref[...] = w2_ref[...].astype(w2c_ref.dtype)

    @pl.when(i >= -1)
    def _():
        base = CONV_HALO - (CONV_WIDTH - 1)
        for l in range(n_slabs):
            lanes = slice(l * LANES, (l + 1) * LANES)
            acc = buf_ref[l, base:base + tm, :] * dw_ref[0:1, lanes]
            for k in range(1, CONV_WIDTH):
                acc = acc + buf_ref[l, base + k:base + k + tm, :] * dw_ref[k:k + 1, lanes]
            o_ref[0, :, lanes] = acc + dwb_ref[j:j + 1, lanes]
        buf_ref[:, 0:CONV_HALO, :] = buf_ref[:, tm:tm + CONV_HALO, :]

    @pl.when(i >= -2)
    def _():
        c = _layer_norm(o_ref[0], _vec(lng_ref, j), _vec(lnb_ref, j))
        c = c * jax.nn.sigmoid(c)
        y = _dot(c, wout_ref[...]) + _vec(bout_ref, j)
        o_ref[0] = _layer_norm(ALPHA * x_ref[0] + y, _vec(g_ref, i_layer), _vec(b_ref, i_layer))


def _layer_block(layer, shape):
    nd = len(shape)
    return pl.BlockSpec((None,) + tuple(shape), lambda *_: (layer,) + (0,) * nd,
                        pipeline_mode=pl.Buffered(1))


def _whole(arr):
    nd = arr.ndim
    return pl.BlockSpec(arr.shape, lambda *_: (0,) * nd, pipeline_mode=pl.Buffered(1))


def _params(*dims):
    return pltpu.CompilerParams(dimension_semantics=dims, vmem_limit_bytes=VMEM_LIMIT_BYTES)


def _pool_mlp_layer(x, j, i, pool_w, pool_scale, mix_g, mix_b, w1, b1, w2, b2, ln_g, ln_b):
    b, s, d = x.shape
    f = w1.shape[2]
    tm = TOKEN_TILE
    tile = pl.BlockSpec((1, tm, d), lambda bi, ti: (bi, ti, 0))
    return pl.pallas_call(
        functools.partial(_pool_mlp_kernel, j, i),
        grid=(b, s // tm),
        in_specs=[tile, _layer_block(j, pool_w.shape[1:]), _whole(pool_scale), _whole(mix_g), _whole(mix_b),
                  _layer_block(i, (d, f)), _whole(b1), _layer_block(i, (f, d)), _whole(b2),
                  _whole(ln_g), _whole(ln_b)],
        out_specs=tile,
        out_shape=jax.ShapeDtypeStruct(x.shape, x.dtype),
        scratch_shapes=[pltpu.VMEM((d // LANES, POOL_HALO + tm, LANES), jnp.float32),
                        pltpu.VMEM((2, POOL_HALO + tm, LANES), jnp.float32),
                        pltpu.VMEM((tm, d), jnp.float32),
                        pltpu.VMEM((tm, f), jnp.float32)],
        compiler_params=_params("arbitrary", "arbitrary"),
        name="pool_mlp",
    )(x, pool_w, pool_scale, mix_g, mix_b, w1, b1, w2, b2, ln_g, ln_b)


def _mlp_layer(x, i, w1, b1, w2, b2, ln_g, ln_b):
    b, s, d = x.shape
    f = w1.shape[1]
    tm = BF16_MLP_TOKEN_TILE
    n = b * s
    tile = pl.BlockSpec((tm, d), lambda ti: (ti, 0))
    out = pl.pallas_call(
        functools.partial(_mlp_kernel, i),
        grid=(n // tm,),
        in_specs=[tile, _whole(w1), _whole(b1), _whole(w2), _whole(b2), _whole(ln_g), _whole(ln_b)],
        out_specs=tile,
        out_shape=jax.ShapeDtypeStruct((n, d), x.dtype),
        scratch_shapes=[pltpu.VMEM((tm, f), w1.dtype)],
        compiler_params=_params("arbitrary"),
        name="sqrelu_mlp",
    )(x.reshape(n, d), w1, b1, w2, b2, ln_g, ln_b)
    return out.reshape(b, s, d)


def _conv_layer(x, j, i, w_in, b_in, dw, dw_b, ln_g, ln_b, w_out, b_out, mix_g, mix_b, mlp_w1, mlp_w2):
    b, s, d = x.shape
    f = mlp_w1.shape[2]
    tm = TOKEN_TILE
    tiles = s // tm
    steps = b * tiles
    tile = pl.BlockSpec((1, tm, d), lambda bi, ti: (bi, ti, 0))
    w1_rows, w2_rows = d // steps, f // steps
    return pl.pallas_call(
        functools.partial(_conv_kernel, j, i),
        grid=(b, tiles),
        in_specs=[tile, _layer_block(j, (d, 2 * d)), _whole(b_in), _layer_block(j, dw.shape[1:]),
                  _whole(dw_b), _whole(ln_g), _whole(ln_b), _layer_block(j, (d, d)), _whole(b_out),
                  _whole(mix_g), _whole(mix_b),
                  pl.BlockSpec((None, w1_rows, f), lambda bi, ti: (i, bi * tiles + ti, 0)),
                  pl.BlockSpec((None, w2_rows, d), lambda bi, ti: (i, bi * tiles + ti, 0))],
        out_specs=[tile,
                   pl.BlockSpec((w1_rows, f), lambda bi, ti: (bi * tiles + ti, 0)),
                   pl.BlockSpec((w2_rows, d), lambda bi, ti: (bi * tiles + ti, 0))],
        out_shape=[jax.ShapeDtypeStruct(x.shape, x.dtype),
                   jax.ShapeDtypeStruct((d, f), jnp.bfloat16),
                   jax.ShapeDtypeStruct((f, d), jnp.bfloat16)],
        scratch_shapes=[pltpu.VMEM((d // LANES, CONV_HALO + tm, LANES), jnp.float32)],
        compiler_params=_params("arbitrary", "arbitrary"),
        name="conv_module",
    )(x, w_in, b_in, dw, dw_b, ln_g, ln_b, w_out, b_out, mix_g, mix_b, mlp_w1, mlp_w2)


def kernel(x, pool_w, pool_scale, conv_w_in, conv_b_in, conv_dw, conv_dw_b, conv_ln_g, conv_ln_b,
           conv_w_out, conv_b_out, mix_ln_g, mix_ln_b, mlp_w1, mlp_b1, mlp_w2, mlp_b2, mlp_ln_g, mlp_ln_b):
    for i in range(DEPTH):
        j = i // 2
        if i % 2 == 0:
            x = _pool_mlp_layer(x, j, i, pool_w, pool_scale, mix_ln_g, mix_ln_b,
                                mlp_w1, mlp_b1, mlp_w2, mlp_b2, mlp_ln_g, mlp_ln_b)
        else:
            x, w1c, w2c = _conv_layer(x, j, i, conv_w_in, conv_b_in, conv_dw, conv_dw_b, conv_ln_g,
                                      conv_ln_b, conv_w_out, conv_b_out, mix_ln_g, mix_ln_b,
                                      mlp_w1, mlp_w2)
            x = _mlp_layer(x, i, w1c, mlp_b1, w2c, mlp_b2, mlp_ln_g, mlp_ln_b)
    return x
```

```python
import functools

import jax
import jax.numpy as jnp
from jax.experimental import pallas as pl
from jax.experimental.pallas import tpu as pltpu

POOL_WINDOWS = (2, 4, 8, 16)
CONV_WIDTH = 31
DEPTH = 2
ALPHA = (2.0 * DEPTH) ** 0.25
LN_EPS = 1e-5

SUBLANES = 8
LANES = 128
POOL_HALO = 32
CONV_HALO = 32

TOKEN_TILE = 512
STREAM_TOKEN_TILE = 2048
STREAM_F_BLOCK = 1024
STREAM_F_CHUNK = 512
MLP_ROWS = 256
GLU_ROWS = 256
VMEM_LIMIT_BYTES = 56 * 1024 * 1024
STREAM_VMEM_LIMIT_BYTES = 60 * 1024 * 1024


def _layer_norm(z, g, b):
    mu = jnp.mean(z, axis=-1, keepdims=True)
    zc = z - mu
    var = jnp.mean(zc * zc, axis=-1, keepdims=True)
    return zc * jax.lax.rsqrt(var + LN_EPS) * g + b


def _dot(a, b):
    return jnp.dot(a, b, preferred_element_type=jnp.float32)


def _vec(ref, layer):
    return ref[layer:layer + 1, :]


def _window_sum(buf_ref, tmp_ref, l, w, tm):
    levels = w.bit_length() - 1
    assert w == 1 << levels and SUBLANES * (levels - 1) + 1 <= POOL_HALO
    src = buf_ref.at[l]
    end = POOL_HALO + tm
    for k in range(1, levels + 1):
        shift = 1 << (k - 1)
        start = POOL_HALO - SUBLANES * (levels - k)
        cur = src[start:end, :] + src[start - shift:end - shift, :]
        if k == levels:
            return cur
        tmp_ref[k % 2, start:end, :] = cur
        src = tmp_ref.at[k % 2]


def _pool_tile(j, i_layer, x_ref, w_ref, scale_ref, g_ref, b_ref, dst_ref, buf_ref, tmp_ref):
    i = pl.program_id(1)
    tm = x_ref.shape[1]
    dg = w_ref.shape[1]
    n_slabs = buf_ref.shape[0]
    slabs_per_group = dg // LANES

    @pl.when(i == 0)
    def _():
        buf_ref[:, 0:POOL_HALO, :] = jnp.zeros((n_slabs, POOL_HALO, LANES), jnp.float32)

    x = x_ref[0]
    for l in range(n_slabs):
        buf_ref[l, POOL_HALO:, :] = x[:, l * LANES:(l + 1) * LANES]
    t = i * tm + jax.lax.broadcasted_iota(jnp.int32, (tm, 1), 0)
    tf = (t + 1).astype(jnp.float32)
    for g, w in enumerate(POOL_WINDOWS):
        cols = slice(g * dg, (g + 1) * dg)
        xg = x[:, cols]
        parts = [_window_sum(buf_ref, tmp_ref, l, w, tm)
                 for l in range(g * slabs_per_group, (g + 1) * slabs_per_group)]
        s = jnp.concatenate(parts, axis=-1)
        count = jnp.minimum(tf, float(w))
        d = s / count - xg
        z = ALPHA * xg + _dot(d, w_ref[g]) * scale_ref[j:j + 1, cols]
        dst_ref[:, cols] = z
    buf_ref[:, 0:POOL_HALO, :] = buf_ref[:, tm:tm + POOL_HALO, :]
    dst_ref[...] = _layer_norm(dst_ref[...], _vec(g_ref, i_layer), _vec(b_ref, i_layer))


def _mlp_tile(i_layer, x_ref, w1_ref, b1_ref, w2_ref, b2_ref, g_ref, b_ref, o_ref, h_ref):
    tm = x_ref.shape[0]
    for r0 in range(0, tm, MLP_ROWS):
        rows = slice(r0, r0 + MLP_ROWS)
        xw = x_ref[rows, :].astype(w1_ref.dtype)
        h = jnp.maximum(_dot(xw, w1_ref[...]) + _vec(b1_ref, i_layer), 0.0)
        h_ref[rows, :] = (h * h).astype(h_ref.dtype)
    for r0 in range(0, tm, MLP_ROWS):
        rows = slice(r0, r0 + MLP_ROWS)
        y = _dot(h_ref[rows, :], w2_ref[...]) + _vec(b2_ref, i_layer)
        o_ref[rows, :] = _layer_norm(ALPHA * x_ref[rows, :] + y, _vec(g_ref, i_layer), _vec(b_ref, i_layer))


def _mlp_stream_kernel(i_layer, x_ref, w1_ref, b1_ref, w2_ref, b2_ref, g_ref, b_ref, o_ref, acc_ref, h_ref):
    c = pl.program_id(1)
    last = pl.num_programs(1) - 1
    tm = x_ref.shape[0]
    chunks = [slice(a, a + STREAM_F_CHUNK) for a in range(0, w1_ref.shape[1], STREAM_F_CHUNK)]

    def hidden():
        xw = x_ref[...].astype(w1_ref.dtype)
        for cols in chunks:
            h = jnp.maximum(_dot(xw, w1_ref[:, cols]) + b1_ref[i_layer:i_layer + 1, cols], 0.0)
            h_ref[:, cols] = (h * h).astype(h_ref.dtype)

    @pl.when(c == 0)
    def _():
        hidden()
        acc_ref[...] = _dot(h_ref[...], w2_ref[...])

    @pl.when(jnp.logical_and(c > 0, c < last))
    def _():
        hidden()
        for cols in chunks:
            acc_ref[...] += _dot(h_ref[:, cols], w2_ref[cols, :])

    @pl.when(c == last)
    def _():
        hidden()
        for r0 in range(0, tm, MLP_ROWS):
            rows = slice(r0, r0 + MLP_ROWS)
            y = acc_ref[rows, :] + _dot(h_ref[rows, :], w2_ref[...]) + _vec(b2_ref, i_layer)
            o_ref[rows, :] = _layer_norm(ALPHA * x_ref[rows, :] + y, _vec(g_ref, i_layer), _vec(b_ref, i_layer))


def _pool_mlp_kernel(j, i_layer, x_ref, pw_ref, scale_ref, mg_ref, mb_ref, w1_ref, b1_ref, w2_ref, b2_ref,
                     g_ref, b_ref, o_ref, buf_ref, tmp_ref, mix_ref, h_ref):
    i = pl.program_id(1)

    @pl.when(i >= 0)
    def _():
        _pool_tile(j, i_layer, x_ref, pw_ref, scale_ref, mg_ref, mb_ref, mix_ref, buf_ref, tmp_ref)

    @pl.when(i >= -1)
    def _():
        _mlp_tile(i_layer, mix_ref, w1_ref, b1_ref, w2_ref, b2_ref, g_ref, b_ref, o_ref.at[0], h_ref)


def _conv_kernel(j, i_layer, x_ref, win_ref, bin_ref, dw_ref, dwb_ref, lng_ref, lnb_ref, wout_ref,
                 bout_ref, g_ref, b_ref, w1_ref, w2_ref, o_ref, w1c_ref, w2c_ref, buf_ref):
    i = pl.program_id(1)
    tm = x_ref.shape[1]
    d = x_ref.shape[2]
    n_slabs = buf_ref.shape[0]

    @pl.when(i == 0)
    def _():
        buf_ref[:, 0:CONV_HALO, :] = jnp.zeros((n_slabs, CONV_HALO, LANES), jnp.float32)

    @pl.when(i >= 0)
    def _():
        for r0 in range(0, tm, GLU_ROWS):
            h = _dot(x_ref[0, r0:r0 + GLU_ROWS, :], win_ref[...]) + _vec(bin_ref, j)
            glu = h[:, :d] * jax.nn.sigmoid(h[:, d:])
            for l in range(n_slabs):
                buf_ref[l, CONV_HALO + r0:CONV_HALO + r0 + GLU_ROWS, :] = glu[:, l * LANES:(l + 1) * LANES]
        w1c_ref[...] = w1_ref[...].astype(w1c_ref.dtype)
        w2c_ref[...] = w2_ref[...].astype(w2c_ref.dtype)

    @pl.when(i >= -1)
    def _():
        base = CONV_HALO - (CONV_WIDTH - 1)
        for l in range(n_slabs):
            lanes = slice(l * LANES, (l + 1) * LANES)
            acc = buf_ref[l, base:base + tm, :] * dw_ref[0:1, lanes]
            for k in range(1, CONV_WIDTH):
                acc = acc + buf_ref[l, base + k:base + k + tm, :] * dw_ref[k:k + 1, lanes]
            o_ref[0, :, lanes] = acc + dwb_ref[j:j + 1, lanes]
        buf_ref[:, 0:CONV_HALO, :] = buf_ref[:, tm:tm + CONV_HALO, :]

    @pl.when(i >= -2)
    def _():
        c = _layer_norm(o_ref[0], _vec(lng_ref, j), _vec(lnb_ref, j))
        c = c * jax.nn.sigmoid(c)
        y = _dot(c, wout_ref[...]) + _vec(bout_ref, j)
        o_ref[0] = _layer_norm(ALPHA * x_ref[0] + y, _vec(g_ref, i_layer), _vec(b_ref, i_layer))


def _layer_block(layer, shape):
    nd = len(shape)
    return pl.BlockSpec((None,) + tuple(shape), lambda *_: (layer,) + (0,) * nd,
                        pipeline_mode=pl.Buffered(1))


def _whole(arr):
    nd = arr.ndim
    return pl.BlockSpec(arr.shape, lambda *_: (0,) * nd, pipeline_mode=pl.Buffered(1))


def _params(*dims):
    return pltpu.CompilerParams(dimension_semantics=dims, vmem_limit_bytes=VMEM_LIMIT_BYTES)


def _pool_mlp_layer(x, j, i, pool_w, pool_scale, mix_g, mix_b, w1, b1, w2, b2, ln_g, ln_b):
    b, s, d = x.shape
    f = w1.shape[2]
    tm = TOKEN_TILE
    tile = pl.BlockSpec((1, tm, d), lambda bi, ti: (bi, ti, 0))
    return pl.pallas_call(
        functools.partial(_pool_mlp_kernel, j, i),
        grid=(b, s // tm),
        in_specs=[tile, _layer_block(j, pool_w.shape[1:]), _whole(pool_scale), _whole(mix_g), _whole(mix_b),
                  _layer_block(i, (d, f)), _whole(b1), _layer_block(i, (f, d)), _whole(b2),
                  _whole(ln_g), _whole(ln_b)],
        out_specs=tile,
        out_shape=jax.ShapeDtypeStruct(x.shape, x.dtype),
        scratch_shapes=[pltpu.VMEM((d // LANES, POOL_HALO + tm, LANES), jnp.float32),
                        pltpu.VMEM((2, POOL_HALO + tm, LANES), jnp.float32),
                        pltpu.VMEM((tm, d), jnp.float32),
                        pltpu.VMEM((tm, f), jnp.float32)],
        compiler_params=_params("arbitrary", "arbitrary"),
        name="pool_mlp",
    )(x, pool_w, pool_scale, mix_g, mix_b, w1, b1, w2, b2, ln_g, ln_b)


def _mlp_stream_layer(x, i, w1, b1, w2, b2, ln_g, ln_b):
    b, s, d = x.shape
    f = w1.shape[1]
    tm, fb = STREAM_TOKEN_TILE, STREAM_F_BLOCK
    assert f // fb >= 2
    n = b * s
    tile = pl.BlockSpec((tm, d), lambda ti, ci: (ti, 0))
    out = pl.pallas_call(
        functools.partial(_mlp_stream_kernel, i),
        grid=(n // tm, f // fb),
        in_specs=[tile, pl.BlockSpec((d, fb), lambda ti, ci: (0, ci)),
                  pl.BlockSpec((b1.shape[0], fb), lambda ti, ci: (0, ci)),
                  pl.BlockSpec((fb, d), lambda ti, ci: (ci, 0)), _whole(b2), _whole(ln_g), _whole(ln_b)],
        out_specs=tile,
        out_shape=jax.ShapeDtypeStruct((n, d), x.dtype),
        scratch_shapes=[pltpu.VMEM((tm, d), jnp.float32), pltpu.VMEM((tm, fb), w1.dtype)],
        compiler_params=pltpu.CompilerParams(dimension_semantics=("arbitrary", "arbitrary"),
                                             vmem_limit_bytes=STREAM_VMEM_LIMIT_BYTES),
        name="sqrelu_mlp",
    )(x.reshape(n, d), w1, b1, w2, b2, ln_g, ln_b)
    return out.reshape(b, s, d)


def _conv_layer(x, j, i, w_in, b_in, dw, dw_b, ln_g, ln_b, w_out, b_out, mix_g, mix_b, mlp_w1, mlp_w2):
    b, s, d = x.shape
    f = mlp_w1.shape[2]
    tm = TOKEN_TILE
    tiles = s // tm
    steps = b * tiles
    tile = pl.BlockSpec((1, tm, d), lambda bi, ti: (bi, ti, 0))
    w1_rows, w2_rows = d // steps, f // steps
    return pl.pallas_call(
        functools.partial(_conv_kernel, j, i),
        grid=(b, tiles),
        in_specs=[tile, _layer_block(j, (d, 2 * d)), _whole(b_in), _layer_block(j, dw.shape[1:]),
                  _whole(dw_b), _whole(ln_g), _whole(ln_b), _layer_block(j, (d, d)), _whole(b_out),
                  _whole(mix_g), _whole(mix_b),
                  pl.BlockSpec((None, w1_rows, f), lambda bi, ti: (i, bi * tiles + ti, 0)),
                  pl.BlockSpec((None, w2_rows, d), lambda bi, ti: (i, bi * tiles + ti, 0))],
        out_specs=[tile,
                   pl.BlockSpec((w1_rows, f), lambda bi, ti: (bi * tiles + ti, 0)),
                   pl.BlockSpec((w2_rows, d), lambda bi, ti: (bi * tiles + ti, 0))],
        out_shape=[jax.ShapeDtypeStruct(x.shape, x.dtype),
                   jax.ShapeDtypeStruct((d, f), jnp.bfloat16),
                   jax.ShapeDtypeStruct((f, d), jnp.bfloat16)],
        scratch_shapes=[pltpu.VMEM((d // LANES, CONV_HALO + tm, LANES), jnp.float32)],
        compiler_params=_params("arbitrary", "arbitrary"),
        name="conv_module",
    )(x, w_in, b_in, dw, dw_b, ln_g, ln_b, w_out, b_out, mix_g, mix_b, mlp_w1, mlp_w2)


def kernel(x, pool_w, pool_scale, conv_w_in, conv_b_in, conv_dw, conv_dw_b, conv_ln_g, conv_ln_b,
           conv_w_out, conv_b_out, mix_ln_g, mix_ln_b, mlp_w1, mlp_b1, mlp_w2, mlp_b2, mlp_ln_g, mlp_ln_b):
    for i in range(DEPTH):
        j = i // 2
        if i % 2 == 0:
            x = _pool_mlp_layer(x, j, i, pool_w, pool_scale, mix_ln_g, mix_ln_b,
                                mlp_w1, mlp_b1, mlp_w2, mlp_b2, mlp_ln_g, mlp_ln_b)
        else:
            x, w1c, w2c = _conv_layer(x, j, i, conv_w_in, conv_b_in, conv_dw, conv_dw_b, conv_ln_g,
                                      conv_ln_b, conv_w_out, conv_b_out, mix_ln_g, mix_ln_b,
                                      mlp_w1, mlp_w2)
            x = _mlp_stream_layer(x, i, w1c, mlp_b1, w2c, mlp_b2, mlp_ln_g, mlp_ln_b)
    return x
```

```python
import functools

import jax
import jax.numpy as jnp
from jax.experimental import pallas as pl
from jax.experimental.pallas import tpu as pltpu

POOL_WINDOWS = (2, 4, 8, 16)
CONV_WIDTH = 31
DEPTH = 2
ALPHA = (2.0 * DEPTH) ** 0.25
LN_EPS = 1e-5

SUBLANES = 8
LANES = 128
POOL_HALO = 32
CONV_HALO = 32

TOKEN_TILE = 512
BF16_MLP_TOKEN_TILE = 1024
MLP_ROWS = 256
GLU_ROWS = 256
VMEM_LIMIT_BYTES = 56 * 1024 * 1024


def _layer_norm(z, g, b):
    mu = jnp.mean(z, axis=-1, keepdims=True)
    zc = z - mu
    var = jnp.mean(zc * zc, axis=-1, keepdims=True)
    return zc * jax.lax.rsqrt(var + LN_EPS) * g + b


def _dot(a, b):
    return jnp.dot(a, b, preferred_element_type=jnp.float32)


def _vec(ref, layer):
    return ref[layer:layer + 1, :]


def _window_sum(buf_ref, tmp_ref, l, w, tm):
    levels = w.bit_length() - 1
    assert w == 1 << levels and SUBLANES * (levels - 1) + 1 <= POOL_HALO
    src = buf_ref.at[l]
    end = POOL_HALO + tm
    for k in range(1, levels + 1):
        shift = 1 << (k - 1)
        start = POOL_HALO - SUBLANES * (levels - k)
        cur = src[start:end, :] + src[start - shift:end - shift, :]
        if k == levels:
            return cur
        tmp_ref[k % 2, start:end, :] = cur
        src = tmp_ref.at[k % 2]


def _pool_tile(j, i_layer, x_ref, w_ref, scale_ref, g_ref, b_ref, dst_ref, buf_ref, tmp_ref):
    i = pl.program_id(1)
    tm = x_ref.shape[1]
    dg = w_ref.shape[1]
    n_slabs = buf_ref.shape[0]
    slabs_per_group = dg // LANES

    @pl.when(i == 0)
    def _():
        buf_ref[:, 0:POOL_HALO, :] = jnp.zeros((n_slabs, POOL_HALO, LANES), jnp.float32)

    x = x_ref[0]
    for l in range(n_slabs):
        buf_ref[l, POOL_HALO:, :] = x[:, l * LANES:(l + 1) * LANES]
    t = i * tm + jax.lax.broadcasted_iota(jnp.int32, (tm, 1), 0)
    tf = (t + 1).astype(jnp.float32)
    for g, w in enumerate(POOL_WINDOWS):
        cols = slice(g * dg, (g + 1) * dg)
        xg = x[:, cols]
        parts = [_window_sum(buf_ref, tmp_ref, l, w, tm)
                 for l in range(g * slabs_per_group, (g + 1) * slabs_per_group)]
        s = jnp.concatenate(parts, axis=-1)
        count = jnp.minimum(tf, float(w))
        d = s / count - xg
        z = ALPHA * xg + _dot(d, w_ref[g]) * scale_ref[j:j + 1, cols]
        dst_ref[:, cols] = z
    buf_ref[:, 0:POOL_HALO, :] = buf_ref[:, tm:tm + POOL_HALO, :]
    dst_ref[...] = _layer_norm(dst_ref[...], _vec(g_ref, i_layer), _vec(b_ref, i_layer))


def _mlp_tile(i_layer, x_ref, w1_ref, b1_ref, w2_ref, b2_ref, g_ref, b_ref, o_ref, h_ref):
    tm = x_ref.shape[0]
    for r0 in range(0, tm, MLP_ROWS):
        rows = slice(r0, r0 + MLP_ROWS)
        xw = x_ref[rows, :].astype(w1_ref.dtype)
        h = jnp.maximum(_dot(xw, w1_ref[...]) + _vec(b1_ref, i_layer), 0.0)
        h_ref[rows, :] = (h * h).astype(h_ref.dtype)
    for r0 in range(0, tm, MLP_ROWS):
        rows = slice(r0, r0 + MLP_ROWS)
        y = _dot(h_ref[rows, :], w2_ref[...]) + _vec(b2_ref, i_layer)
        o_ref[rows, :] = _layer_norm(ALPHA * x_ref[rows, :] + y, _vec(g_ref, i_layer), _vec(b_ref, i_layer))


def _mlp_kernel(i_layer, x_ref, w1_ref, b1_ref, w2_ref, b2_ref, g_ref, b_ref, o_ref, h_ref):
    n_chunks = x_ref.shape[0] // MLP_ROWS

    def hidden(c):
        rows = slice(c * MLP_ROWS, (c + 1) * MLP_ROWS)
        xw = x_ref[rows, :].astype(w1_ref.dtype)
        h = jnp.maximum(_dot(xw, w1_ref[...]) + _vec(b1_ref, i_layer), 0.0)
        h_ref[c % 2] = (h * h).astype(h_ref.dtype)

    def output(c):
        rows = slice(c * MLP_ROWS, (c + 1) * MLP_ROWS)
        y = _dot(h_ref[c % 2], w2_ref[...]) + _vec(b2_ref, i_layer)
        o_ref[rows, :] = _layer_norm(ALPHA * x_ref[rows, :] + y, _vec(g_ref, i_layer), _vec(b_ref, i_layer))

    hidden(0)
    for c in range(n_chunks):
        if c + 1 < n_chunks:
            hidden(c + 1)
        output(c)


def _pool_mlp_kernel(j, i_layer, x_ref, pw_ref, scale_ref, mg_ref, mb_ref, w1_ref, b1_ref, w2_ref, b2_ref,
                     g_ref, b_ref, o_ref, buf_ref, tmp_ref, mix_ref, h_ref):
    i = pl.program_id(1)

    @pl.when(i >= 0)
    def _():
        _pool_tile(j, i_layer, x_ref, pw_ref, scale_ref, mg_ref, mb_ref, mix_ref, buf_ref, tmp_ref)

    @pl.when(i >= -1)
    def _():
        _mlp_tile(i_layer, mix_ref, w1_ref, b1_ref, w2_ref, b2_ref, g_ref, b_ref, o_ref.at[0], h_ref)


def _conv_kernel(j, i_layer, x_ref, win_ref, bin_ref, dw_ref, dwb_ref, lng_ref, lnb_ref, wout_ref,
                 bout_ref, g_ref, b_ref, w1_ref, w2_ref, o_ref, w1c_ref, w2c_ref, buf_ref):
    i = pl.program_id(1)
    tm = x_ref.shape[1]
    d = x_ref.shape[2]
    n_slabs = buf_ref.shape[0]

    @pl.when(i == 0)
    def _():
        buf_ref[:, 0:CONV_HALO, :] = jnp.zeros((n_slabs, CONV_HALO, LANES), jnp.float32)

    @pl.when(i >= 0)
    def _():
        for r0 in range(0, tm, GLU_ROWS):
            h = _dot(x_ref[0, r0:r0 + GLU_ROWS, :], win_ref[...]) + _vec(bin_ref, j)
            glu = h[:, :d] * jax.nn.sigmoid(h[:, d:])
            for l in range(n_slabs):
                buf_ref[l, CONV_HALO + r0:CONV_HALO + r0 + GLU_ROWS, :] = glu[:, l * LANES:(l + 1) * LANES]
        w1c_ref[...] = w1_ref[...].astype(w1c_ref.dtype)
        w2c_ref[...] = w2_ref[...].astype(w2c_ref.dtype)

    @pl.when(i >= -1)
    def _():
        base = CONV_HALO - (CONV_WIDTH - 1)
        for l in range(n_slabs):
            lanes = slice(l * LANES, (l + 1) * LANES)
            acc = buf_ref[l, base:base + tm, :] * dw_ref[0:1, lanes]
            for k in range(1, CONV_WIDTH):
                acc = acc + buf_ref[l, base + k:base + k + tm, :] * dw_ref[k:k + 1, lanes]
            o_ref[0, :, lanes] = acc + dwb_ref[j:j + 1, lanes]
        buf_ref[:, 0:CONV_HALO, :] = buf_ref[:, tm:tm + CONV_HALO, :]

    @pl.when(i >= -2)
    def _():
        c = _layer_norm(o_ref[0], _vec(lng_ref, j), _vec(lnb_ref, j))
        c = c * jax.nn.sigmoid(c)
        y = _dot(c, wout_ref[...]) + _vec(bout_ref, j)
        o_ref[0] = _layer_norm(ALPHA * x_ref[0] + y, _vec(g_ref, i_layer), _vec(b_ref, i_layer))


def _layer_block(layer, shape):
    nd = len(shape)
    return pl.BlockSpec((None,) + tuple(shape), lambda *_: (layer,) + (0,) * nd,
                        pipeline_mode=pl.Buffered(1))


def _whole(arr):
    nd = arr.ndim
    return pl.BlockSpec(arr.shape, lambda *_: (0,) * nd, pipeline_mode=pl.Buffered(1))


def _params(*dims):
    return pltpu.CompilerParams(dimension_semantics=dims, vmem_limit_bytes=VMEM_LIMIT_BYTES)


def _pool_mlp_layer(x, j, i, pool_w, pool_scale, mix_g, mix_b, w1, b1, w2, b2, ln_g, ln_b):
    b, s, d = x.shape
    f = w1.shape[2]
    tm = TOKEN_TILE
    tile = pl.BlockSpec((1, tm, d), lambda bi, ti: (bi, ti, 0))
    return pl.pallas_call(
        functools.partial(_pool_mlp_kernel, j, i),
        grid=(b, s // tm),
        in_specs=[tile, _layer_block(j, pool_w.shape[1:]), _whole(pool_scale), _whole(mix_g), _whole(mix_b),
                  _layer_block(i, (d, f)), _whole(b1), _layer_block(i, (f, d)), _whole(b2),
                  _whole(ln_g), _whole(ln_b)],
        out_specs=tile,
        out_shape=jax.ShapeDtypeStruct(x.shape, x.dtype),
        scratch_shapes=[pltpu.VMEM((d // LANES, POOL_HALO + tm, LANES), jnp.float32),
                        pltpu.VMEM((2, POOL_HALO + tm, LANES), jnp.float32),
                        pltpu.VMEM((tm, d), jnp.float32),
                        pltpu.VMEM((tm, f), jnp.float32)],
        compiler_params=_params("arbitrary", "arbitrary"),
        name="pool_mlp",
    )(x, pool_w, pool_scale, mix_g, mix_b, w1, b1, w2, b2, ln_g, ln_b)


def _mlp_layer(x, i, w1, b1, w2, b2, ln_g, ln_b):
    b, s, d = x.shape
    f = w1.shape[1]
    tm = BF16_MLP_TOKEN_TILE
    n = b * s
    tile = pl.BlockSpec((tm, d), lambda ti: (ti, 0))
    out = pl.pallas_call(
        functools.partial(_mlp_kernel, i),
        grid=(n // tm,),
        in_specs=[tile, _whole(w1), _whole(b1), _whole(w2), _whole(b2), _whole(ln_g), _whole(ln_b)],
        out_specs=tile,
        out_shape=jax.ShapeDtypeStruct((n, d), x.dtype),
        scratch_shapes=[pltpu.VMEM((2, MLP_ROWS, f), w1.dtype)],
        compiler_params=_params("arbitrary"),
        name="sqrelu_mlp",
    )(x.reshape(n, d), w1, b1, w2, b2, ln_g, ln_b)
    return out.reshape(b, s, d)


def _conv_layer(x, j, i, w_in, b_in, dw, dw_b, ln_g, ln_b, w_out, b_out, mix_g, mix_b, mlp_w1, mlp_w2):
    b, s, d = x.shape
    f = mlp_w1.shape[2]
    tm = TOKEN_TILE
    tiles = s // tm
    steps = b * tiles
    tile = pl.BlockSpec((1, tm, d), lambda bi, ti: (bi, ti, 0))
    w1_rows, w2_rows = d // steps, f // steps
    return pl.pallas_call(
        functools.partial(_conv_kernel, j, i),
        grid=(b, tiles),
        in_specs=[tile, _layer_block(j, (d, 2 * d)), _whole(b_in), _layer_block(j, dw.shape[1:]),
                  _whole(dw_b), _whole(ln_g), _whole(ln_b), _layer_block(j, (d, d)), _whole(b_out),
                  _whole(mix_g), _whole(mix_b),
                  pl.BlockSpec((None, w1_rows, f), lambda bi, ti: (i, bi * tiles + ti, 0)),
                  pl.BlockSpec((None, w2_rows, d), lambda bi, ti: (i, bi * tiles + ti, 0))],
        out_specs=[tile,
                   pl.BlockSpec((w1_rows, f), lambda bi, ti: (bi * tiles + ti, 0)),
                   pl.BlockSpec((w2_rows, d), lambda bi, ti: (bi * tiles + ti, 0))],
        out_shape=[jax.ShapeDtypeStruct(x.shape, x.dtype),
                   jax.ShapeDtypeStruct((d, f), jnp.bfloat16),
                   jax.ShapeDtypeStruct((f, d), jnp.bfloat16)],
        scratch_shapes=[pltpu.VMEM((d // LANES, CONV_HALO + tm, LANES), jnp.float32)],
        compiler_params=_params("arbitrary", "arbitrary"),
        name="conv_module",
    )(x, w_in, b_in, dw, dw_b, ln_g, ln_b, w_out, b_out, mix_g, mix_b, mlp_w1, mlp_w2)


def kernel(x, pool_w, pool_scale, conv_w_in, conv_b_in, conv_dw, conv_dw_b, conv_ln_g, conv_ln_b,
           conv_w_out, conv_b_out, mix_ln_g, mix_ln_b, mlp_w1, mlp_b1, mlp_w2, mlp_b2, mlp_ln_g, mlp_ln_b):
    for i in range(DEPTH):
        j = i // 2
        if i % 2 == 0:
            x = _pool_mlp_layer(x, j, i, pool_w, pool_scale, mix_ln_g, mix_ln_b,
                                mlp_w1, mlp_b1, mlp_w2, mlp_b2, mlp_ln_g, mlp_ln_b)
        else:
            x, w1c, w2c = _conv_layer(x, j, i, conv_w_in, conv_b_in, conv_dw, conv_dw_b, conv_ln_g,
                                      conv_ln_b, conv_w_out, conv_b_out, mix_ln_g, mix_ln_b,
                                      mlp_w1, mlp_w2)
            x = _mlp_layer(x, i, w1c, mlp_b1, w2c, mlp_b2, mlp_ln_g, mlp_ln_b)
    return x
```

```python
import functools

import jax
import jax.numpy as jnp
from jax.experimental import pallas as pl
from jax.experimental.pallas import tpu as pltpu

POOL_WINDOWS = (2, 4, 8, 16)
CONV_WIDTH = 31
DEPTH = 2
ALPHA = (2.0 * DEPTH) ** 0.25
LN_EPS = 1e-5

SUBLANES = 8
LANES = 128
POOL_HALO = 32
CONV_HALO = 32

TOKEN_TILE = 512
BF16_MLP_TOKEN_TILE = 1024
MLP_ROWS = 256
GLU_ROWS = 256
VMEM_LIMIT_BYTES = 56 * 1024 * 1024


def _layer_norm(z, g, b):
    mu = jnp.mean(z, axis=-1, keepdims=True)
    zc = z - mu
    var = jnp.mean(zc * zc, axis=-1, keepdims=True)
    return zc * jax.lax.rsqrt(var + LN_EPS) * g + b


def _dot(a, b):
    return jnp.dot(a, b, preferred_element_type=jnp.float32)


def _vec(ref, layer):
    return ref[layer:layer + 1, :]


def _window_sum(buf_ref, tmp_ref, l, w, tm):
    levels = w.bit_length() - 1
    assert w == 1 << levels and SUBLANES * (levels - 1) + 1 <= POOL_HALO
    src = buf_ref.at[l]
    end = POOL_HALO + tm
    for k in range(1, levels + 1):
        shift = 1 << (k - 1)
        start = POOL_HALO - SUBLANES * (levels - k)
        cur = src[start:end, :] + src[start - shift:end - shift, :]
        if k == levels:
            return cur
        tmp_ref[k % 2, start:end, :] = cur
        src = tmp_ref.at[k % 2]


def _pool_tile(j, i_layer, x_ref, w_ref, scale_ref, g_ref, b_ref, dst_ref, buf_ref, tmp_ref):
    i = pl.program_id(1)
    tm = x_ref.shape[1]
    dg = w_ref.shape[1]
    n_slabs = buf_ref.shape[0]
    slabs_per_group = dg // LANES

    @pl.when(i == 0)
    def _():
        buf_ref[:, 0:POOL_HALO, :] = jnp.zeros((n_slabs, POOL_HALO, LANES), jnp.float32)

    x = x_ref[0]
    for l in range(n_slabs):
        buf_ref[l, POOL_HALO:, :] = x[:, l * LANES:(l + 1) * LANES]
    t = i * tm + jax.lax.broadcasted_iota(jnp.int32, (tm, 1), 0)
    tf = (t + 1).astype(jnp.float32)
    for g, w in enumerate(POOL_WINDOWS):
        cols = slice(g * dg, (g + 1) * dg)
        xg = x[:, cols]
        parts = [_window_sum(buf_ref, tmp_ref, l, w, tm)
                 for l in range(g * slabs_per_group, (g + 1) * slabs_per_group)]
        s = jnp.concatenate(parts, axis=-1)
        count = jnp.minimum(tf, float(w))
        d = s / count - xg
        z = ALPHA * xg + _dot(d, w_ref[g]) * scale_ref[j:j + 1, cols]
        dst_ref[:, cols] = z
    buf_ref[:, 0:POOL_HALO, :] = buf_ref[:, tm:tm + POOL_HALO, :]
    dst_ref[...] = _layer_norm(dst_ref[...], _vec(g_ref, i_layer), _vec(b_ref, i_layer))


def _mlp_tile(i_layer, x_ref, w1_ref, b1_ref, w2_ref, b2_ref, g_ref, b_ref, o_ref, h_ref):
    tm = x_ref.shape[0]
    for r0 in range(0, tm, MLP_ROWS):
        rows = slice(r0, r0 + MLP_ROWS)
        xw = x_ref[rows, :].astype(w1_ref.dtype)
        h = jnp.maximum(_dot(xw, w1_ref[...]) + _vec(b1_ref, i_layer), 0.0)
        h_ref[rows, :] = (h * h).astype(h_ref.dtype)
    for r0 in range(0, tm, MLP_ROWS):
        rows = slice(r0, r0 + MLP_ROWS)
        y = _dot(h_ref[rows, :], w2_ref[...]) + _vec(b2_ref, i_layer)
        o_ref[rows, :] = _layer_norm(ALPHA * x_ref[rows, :] + y, _vec(g_ref, i_layer), _vec(b_ref, i_layer))


def _mlp_kernel(i_layer, x_ref, w1_ref, b1_ref, w2_ref, b2_ref, g_ref, b_ref, o_ref, h_ref):
    s = pl.program_id(0)
    tm = x_ref.shape[0]

    @pl.when(s >= 0)
    def _():
        for r0 in range(0, tm, MLP_ROWS):
            rows = slice(r0, r0 + MLP_ROWS)
            xw = x_ref[rows, :].astype(w1_ref.dtype)
            h = jnp.maximum(_dot(xw, w1_ref[...]) + _vec(b1_ref, i_layer), 0.0)
            h_ref[rows, :] = (h * h).astype(h_ref.dtype)

    @pl.when(s >= -1)
    def _():
        for r0 in range(0, tm, MLP_ROWS):
            rows = slice(r0, r0 + MLP_ROWS)
            y = _dot(h_ref[rows, :], w2_ref[...]) + _vec(b2_ref, i_layer)
            o_ref[rows, :] = _layer_norm(ALPHA * x_ref[rows, :] + y, _vec(g_ref, i_layer), _vec(b_ref, i_layer))


def _pool_mlp_kernel(j, i_layer, x_ref, pw_ref, scale_ref, mg_ref, mb_ref, w1_ref, b1_ref, w2_ref, b2_ref,
                     g_ref, b_ref, o_ref, buf_ref, tmp_ref, mix_ref, h_ref):
    i = pl.program_id(1)

    @pl.when(i >= 0)
    def _():
        _pool_tile(j, i_layer, x_ref, pw_ref, scale_ref, mg_ref, mb_ref, mix_ref, buf_ref, tmp_ref)

    @pl.when(i >= -1)
    def _():
        _mlp_tile(i_layer, mix_ref, w1_ref, b1_ref, w2_ref, b2_ref, g_ref, b_ref, o_ref.at[0], h_ref)


def _conv_kernel(j, i_layer, x_ref, win_ref, bin_ref, dw_ref, dwb_ref, lng_ref, lnb_ref, wout_ref,
                 bout_ref, g_ref, b_ref, w1_ref, w2_ref, o_ref, w1c_ref, w2c_ref, buf_ref):
    i = pl.program_id(1)
    tm = x_ref.shape[1]
    d = x_ref.shape[2]
    n_slabs = buf_ref.shape[0]

    @pl.when(i == 0)
    def _():
        buf_ref[:, 0:CONV_HALO, :] = jnp.zeros((n_slabs, CONV_HALO, LANES), jnp.float32)

    @pl.when(i >= 0)
    def _():
        for r0 in range(0, tm, GLU_ROWS):
            h = _dot(x_ref[0, r0:r0 + GLU_ROWS, :], win_ref[...]) + _vec(bin_ref, j)
            glu = h[:, :d] * jax.nn.sigmoid(h[:, d:])
            for l in range(n_slabs):
                buf_ref[l, CONV_HALO + r0:CONV_HALO + r0 + GLU_ROWS, :] = glu[:, l * LANES:(l + 1) * LANES]
        w1c_ref[...] = w1_ref[...].astype(w1c_ref.dtype)
        w2c_ref[...] = w2_ref[...].astype(w2c_ref.dtype)

    @pl.when(i >= -1)
    def _():
        base = CONV_HALO - (CONV_WIDTH - 1)
        for l in range(n_slabs):
            lanes = slice(l * LANES, (l + 1) * LANES)
            acc = buf_ref[l, base:base + tm, :] * dw_ref[0:1, lanes]
            for k in range(1, CONV_WIDTH):
                acc = acc + buf_ref[l, base + k:base + k + tm, :] * dw_ref[k:k + 1, lanes]
            o_ref[0, :, lanes] = acc + dwb_ref[j:j + 1, lanes]
        buf_ref[:, 0:CONV_HALO, :] = buf_ref[:, tm:tm + CONV_HALO, :]

    @pl.when(i >= -2)
    def _():
        c = _layer_norm(o_ref[0], _vec(lng_ref, j), _vec(lnb_ref, j))
        c = c * jax.nn.sigmoid(c)
        y = _dot(c, wout_ref[...]) + _vec(bout_ref, j)
        o_ref[0] = _layer_norm(ALPHA * x_ref[0] + y, _vec(g_ref, i_layer), _vec(b_ref, i_layer))


def _layer_block(layer, shape):
    nd = len(shape)
    return pl.BlockSpec((None,) + tuple(shape), lambda *_: (layer,) + (0,) * nd,
                        pipeline_mode=pl.Buffered(1))


def _whole(arr):
    nd = arr.ndim
    return pl.BlockSpec(arr.shape, lambda *_: (0,) * nd, pipeline_mode=pl.Buffered(1))


def _params(*dims):
    return pltpu.CompilerParams(dimension_semantics=dims, vmem_limit_bytes=VMEM_LIMIT_BYTES)


def _pool_mlp_layer(x, j, i, pool_w, pool_scale, mix_g, mix_b, w1, b1, w2, b2, ln_g, ln_b):
    b, s, d = x.shape
    f = w1.shape[2]
    tm = TOKEN_TILE
    tile = pl.BlockSpec((1, tm, d), lambda bi, ti: (bi, ti, 0))
    return pl.pallas_call(
        functools.partial(_pool_mlp_kernel, j, i),
        grid=(b, s // tm),
        in_specs=[tile, _layer_block(j, pool_w.shape[1:]), _whole(pool_scale), _whole(mix_g), _whole(mix_b),
                  _layer_block(i, (d, f)), _whole(b1), _layer_block(i, (f, d)), _whole(b2),
                  _whole(ln_g), _whole(ln_b)],
        out_specs=tile,
        out_shape=jax.ShapeDtypeStruct(x.shape, x.dtype),
        scratch_shapes=[pltpu.VMEM((d // LANES, POOL_HALO + tm, LANES), jnp.float32),
                        pltpu.VMEM((2, POOL_HALO + tm, LANES), jnp.float32),
                        pltpu.VMEM((tm, d), jnp.float32),
                        pltpu.VMEM((tm, f), jnp.float32)],
        compiler_params=_params("arbitrary", "arbitrary"),
        name="pool_mlp",
    )(x, pool_w, pool_scale, mix_g, mix_b, w1, b1, w2, b2, ln_g, ln_b)


def _mlp_layer(x, i, w1, b1, w2, b2, ln_g, ln_b):
    b, s, d = x.shape
    f = w1.shape[1]
    tm = BF16_MLP_TOKEN_TILE
    n = b * s
    tile = pl.BlockSpec((tm, d), lambda ti: (ti, 0))
    out = pl.pallas_call(
        functools.partial(_mlp_kernel, i),
        grid=(n // tm,),
        in_specs=[tile, _whole(w1), _whole(b1), _whole(w2), _whole(b2), _whole(ln_g), _whole(ln_b)],
        out_specs=tile,
        out_shape=jax.ShapeDtypeStruct((n, d), x.dtype),
        scratch_shapes=[pltpu.VMEM((tm, f), w1.dtype)],
        compiler_params=_params("arbitrary"),
        name="sqrelu_mlp",
    )(x.reshape(n, d), w1, b1, w2, b2, ln_g, ln_b)
    return out.reshape(b, s, d)


def _conv_layer(x, j, i, w_in, b_in, dw, dw_b, ln_g, ln_b, w_out, b_out, mix_g, mix_b, mlp_w1, mlp_w2):
    b, s, d = x.shape
    f = mlp_w1.shape[2]
    tm = TOKEN_TILE
    tiles = s // tm
    steps = b * tiles
    tile = pl.BlockSpec((1, tm, d), lambda bi, ti: (bi, ti, 0))
    w1_rows, w2_rows = d // steps, f // steps
    return pl.pallas_call(
        functools.partial(_conv_kernel, j, i),
        grid=(b, tiles),
        in_specs=[tile, _layer_block(j, (d, 2 * d)), _whole(b_in), _layer_block(j, dw.shape[1:]),
                  _whole(dw_b), _whole(ln_g), _whole(ln_b), _layer_block(j, (d, d)), _whole(b_out),
                  _whole(mix_g), _whole(mix_b),
                  pl.BlockSpec((None, w1_rows, f), lambda bi, ti: (i, bi * tiles + ti, 0)),
                  pl.BlockSpec((None, w2_rows, d), lambda bi, ti: (i, bi * tiles + ti, 0))],
        out_specs=[tile,
                   pl.BlockSpec((w1_rows, f), lambda bi, ti: (bi * tiles + ti, 0)),
                   pl.BlockSpec((w2_rows, d), lambda bi, ti: (bi * tiles + ti, 0))],
        out_shape=[jax.ShapeDtypeStruct(x.shape, x.dtype),
                   jax.ShapeDtypeStruct((d, f), jnp.bfloat16),
                   jax.ShapeDtypeStruct((f, d), jnp.bfloat16)],
        scratch_shapes=[pltpu.VMEM((d // LANES, CONV_HALO + tm, LANES), jnp.float32)],
        compiler_params=_params("arbitrary", "arbitrary"),
        name="conv_module",
    )(x, w_in, b_in, dw, dw_b, ln_g, ln_b, w_out, b_out, mix_g, mix_b, mlp_w1, mlp_w2)


def kernel(x, pool_w, pool_scale, conv_w_in, conv_b_in, conv_dw, conv_dw_b, conv_ln_g, conv_ln_b,
           conv_w_out, conv_b_out, mix_ln_g, mix_ln_b, mlp_w1, mlp_b1, mlp_w2, mlp_b2, mlp_ln_g, mlp_ln_b):
    for i in range(DEPTH):
        j = i // 2
        if i % 2 == 0:
            x = _pool_mlp_layer(x, j, i, pool_w, pool_scale, mix_ln_g, mix_ln_b,
                                mlp_w1, mlp_b1, mlp_w2, mlp_b2, mlp_ln_g, mlp_ln_b)
        else:
            x, w1c, w2c = _conv_layer(x, j, i, conv_w_in, conv_b_in, conv_dw, conv_dw_b, conv_ln_g,
                                      conv_ln_b, conv_w_out, conv_b_out, mix_ln_g, mix_ln_b,
                                      mlp_w1, mlp_w2)
            x = _mlp_layer(x, i, w1c, mlp_b1, w2c, mlp_b2, mlp_ln_g, mlp_ln_b)
    return x
```

```python
import functools

import jax
import jax.numpy as jnp
from jax.experimental import pallas as pl
from jax.experimental.pallas import tpu as pltpu

POOL_WINDOWS = (2, 4, 8, 16)
CONV_WIDTH = 31
DEPTH = 2
ALPHA = (2.0 * DEPTH) ** 0.25
LN_EPS = 1e-5

SUBLANES = 8
LANES = 128
POOL_HALO = 32
CONV_HALO = 32

TOKEN_TILE = 512
BF16_MLP_TOKEN_TILE = 1024
MLP_ROWS = 256
GLU_ROWS = 256
VMEM_LIMIT_BYTES = 56 * 1024 * 1024


def _layer_norm(z, g, b):
    mu = jnp.mean(z, axis=-1, keepdims=True)
    zc = z - mu
    var = jnp.mean(zc * zc, axis=-1, keepdims=True)
    return zc * jax.lax.rsqrt(var + LN_EPS) * g + b


def _dot(a, b):
    return jnp.dot(a, b, preferred_element_type=jnp.float32)


def _vec(ref, layer):
    return ref[layer:layer + 1, :]


def _window_sum(buf_ref, tmp_ref, l, w, tm):
    levels = w.bit_length() - 1
    assert w == 1 << levels and SUBLANES * (levels - 1) + 1 <= POOL_HALO
    src = buf_ref.at[l]
    end = POOL_HALO + tm
    for k in range(1, levels + 1):
        shift = 1 << (k - 1)
        start = POOL_HALO - SUBLANES * (levels - k)
        cur = src[start:end, :] + src[start - shift:end - shift, :]
        if k == levels:
            return cur
        tmp_ref[k % 2, start:end, :] = cur
        src = tmp_ref.at[k % 2]


def _pool_tile(j, i_layer, x_ref, w_ref, scale_ref, g_ref, b_ref, dst_ref, buf_ref, tmp_ref):
    i = pl.program_id(1)
    tm = x_ref.shape[1]
    dg = w_ref.shape[1]
    n_slabs = buf_ref.shape[0]
    slabs_per_group = dg // LANES

    @pl.when(i == 0)
    def _():
        buf_ref[:, 0:POOL_HALO, :] = jnp.zeros((n_slabs, POOL_HALO, LANES), jnp.float32)

    x = x_ref[0]
    for l in range(n_slabs):
        buf_ref[l, POOL_HALO:, :] = x[:, l * LANES:(l + 1) * LANES]
    t = i * tm + jax.lax.broadcasted_iota(jnp.int32, (tm, 1), 0)
    tf = (t + 1).astype(jnp.float32)
    for g, w in enumerate(POOL_WINDOWS):
        cols = slice(g * dg, (g + 1) * dg)
        xg = x[:, cols]
        parts = [_window_sum(buf_ref, tmp_ref, l, w, tm)
                 for l in range(g * slabs_per_group, (g + 1) * slabs_per_group)]
        s = jnp.concatenate(parts, axis=-1)
        count = jnp.minimum(tf, float(w))
        d = s / count - xg
        z = ALPHA * xg + _dot(d, w_ref[g]) * scale_ref[j:j + 1, cols]
        dst_ref[:, cols] = z
    buf_ref[:, 0:POOL_HALO, :] = buf_ref[:, tm:tm + POOL_HALO, :]
    dst_ref[...] = _layer_norm(dst_ref[...], _vec(g_ref, i_layer), _vec(b_ref, i_layer))


def _mlp_tile(i_layer, x_ref, w1_ref, b1_ref, w2_ref, b2_ref, g_ref, b_ref, o_ref, h_ref):
    tm = x_ref.shape[0]
    for r0 in range(0, tm, MLP_ROWS):
        rows = slice(r0, r0 + MLP_ROWS)
        xw = x_ref[rows, :].astype(w1_ref.dtype)
        h = jnp.maximum(_dot(xw, w1_ref[...]) + _vec(b1_ref, i_layer), 0.0)
        h_ref[rows, :] = (h * h).astype(h_ref.dtype)
    for r0 in range(0, tm, MLP_ROWS):
        rows = slice(r0, r0 + MLP_ROWS)
        y = _dot(h_ref[rows, :], w2_ref[...]) + _vec(b2_ref, i_layer)
        o_ref[rows, :] = _layer_norm(ALPHA * x_ref[rows, :] + y, _vec(g_ref, i_layer), _vec(b_ref, i_layer))


def _mlp_kernel(i_layer, x_ref, w1_ref, b1_ref, w2_ref, b2_ref, g_ref, b_ref, o_ref, h_ref):
    _mlp_tile(i_layer, x_ref, w1_ref, b1_ref, w2_ref, b2_ref, g_ref, b_ref, o_ref, h_ref)


def _pool_mlp_kernel(j, i_layer, x_ref, pw_ref, scale_ref, mg_ref, mb_ref, w1_ref, b1_ref, w2_ref, b2_ref,
                     g_ref, b_ref, o_ref, buf_ref, tmp_ref, mix_ref, h_ref):
    _pool_tile(j, i_layer, x_ref, pw_ref, scale_ref, mg_ref, mb_ref, mix_ref, buf_ref, tmp_ref)
    _mlp_tile(i_layer, mix_ref, w1_ref, b1_ref, w2_ref, b2_ref, g_ref, b_ref, o_ref.at[0], h_ref)


def _conv_kernel(j, i_layer, x_ref, win_ref, bin_ref, dw_ref, dwb_ref, lng_ref, lnb_ref, wout_ref,
                 bout_ref, g_ref, b_ref, w1_ref, w2_ref, o_ref, w1c_ref, w2c_ref, buf_ref):
    i = pl.program_id(1)
    tm = x_ref.shape[1]
    d = x_ref.shape[2]
    n_slabs = buf_ref.shape[0]

    @pl.when(i == 0)
    def _():
        buf_ref[:, 0:CONV_HALO, :] = jnp.zeros((n_slabs, CONV_HALO, LANES), jnp.float32)

    @pl.when(i >= 0)
    def _():
        for r0 in range(0, tm, GLU_ROWS):
            h = _dot(x_ref[0, r0:r0 + GLU_ROWS, :], win_ref[...]) + _vec(bin_ref, j)
            glu = h[:, :d] * jax.nn.sigmoid(h[:, d:])
            for l in range(n_slabs):
                buf_ref[l, CONV_HALO + r0:CONV_HALO + r0 + GLU_ROWS, :] = glu[:, l * LANES:(l + 1) * LANES]
        w1c_ref[...] = w1_ref[...].astype(w1c_ref.dtype)
        w2c_ref[...] = w2_ref[...].astype(w2c_ref.dtype)

    @pl.when(i >= -1)
    def _():
        base = CONV_HALO - (CONV_WIDTH - 1)
        for l in range(n_slabs):
            lanes = slice(l * LANES, (l + 1) * LANES)
            acc = buf_ref[l, base:base + tm, :] * dw_ref[0:1, lanes]
            for k in range(1, CONV_WIDTH):
                acc = acc + buf_ref[l, base + k:base + k + tm, :] * dw_ref[k:k + 1, lanes]
            o_ref[0, :, lanes] = acc + dwb_ref[j:j + 1, lanes]
        buf_ref[:, 0:CONV_HALO, :] = buf_ref[:, tm:tm + CONV_HALO, :]

    @pl.when(i >= -2)
    def _():
        c = _layer_norm(o_ref[0], _vec(lng_ref, j), _vec(lnb_ref, j))
        c = c * jax.nn.sigmoid(c)
        y = _dot(c, wout_ref[...]) + _vec(bout_ref, j)
        o_ref[0] = _layer_norm(ALPHA * x_ref[0] + y, _vec(g_ref, i_layer), _vec(b_ref, i_layer))


def _layer_block(layer, shape):
    nd = len(shape)
    return pl.BlockSpec((None,) + tuple(shape), lambda *_: (layer,) + (0,) * nd,
                        pipeline_mode=pl.Buffered(1))


def _whole(arr):
    nd = arr.ndim
    return pl.BlockSpec(arr.shape, lambda *_: (0,) * nd, pipeline_mode=pl.Buffered(1))


def _params(*dims):
    return pltpu.CompilerParams(dimension_semantics=dims, vmem_limit_bytes=VMEM_LIMIT_BYTES)


def _pool_mlp_layer(x, j, i, pool_w, pool_scale, mix_g, mix_b, w1, b1, w2, b2, ln_g, ln_b):
    b, s, d = x.shape
    f = w1.shape[2]
    tm = TOKEN_TILE
    tile = pl.BlockSpec((1, tm, d), lambda bi, ti: (bi, ti, 0))
    return pl.pallas_call(
        functools.partial(_pool_mlp_kernel, j, i),
        grid=(b, s // tm),
        in_specs=[tile, _layer_block(j, pool_w.shape[1:]), _whole(pool_scale), _whole(mix_g), _whole(mix_b),
                  _layer_block(i, (d, f)), _whole(b1), _layer_block(i, (f, d)), _whole(b2),
                  _whole(ln_g), _whole(ln_b)],
        out_specs=tile,
        out_shape=jax.ShapeDtypeStruct(x.shape, x.dtype),
        scratch_shapes=[pltpu.VMEM((d // LANES, POOL_HALO + tm, LANES), jnp.float32),
                        pltpu.VMEM((2, POOL_HALO + tm, LANES), jnp.float32),
                        pltpu.VMEM((tm, d), jnp.float32),
                        pltpu.VMEM((tm, f), jnp.float32)],
        compiler_params=_params("arbitrary", "arbitrary"),
        name="pool_mlp",
    )(x, pool_w, pool_scale, mix_g, mix_b, w1, b1, w2, b2, ln_g, ln_b)


def _mlp_layer(x, i, w1, b1, w2, b2, ln_g, ln_b):
    b, s, d = x.shape
    f = w1.shape[1]
    tm = BF16_MLP_TOKEN_TILE
    n = b * s
    tile = pl.BlockSpec((tm, d), lambda ti: (ti, 0))
    out = pl.pallas_call(
        functools.partial(_mlp_kernel, i),
        grid=(n // tm,),
        in_specs=[tile, _whole(w1), _whole(b1), _whole(w2), _whole(b2), _whole(ln_g), _whole(ln_b)],
        out_specs=tile,
        out_shape=jax.ShapeDtypeStruct((n, d), x.dtype),
        scratch_shapes=[pltpu.VMEM((tm, f), w1.dtype)],
        compiler_params=_params("arbitrary"),
        name="sqrelu_mlp",
    )(x.reshape(n, d), w1, b1, w2, b2, ln_g, ln_b)
    return out.reshape(b, s, d)


def _conv_layer(x, j, i, w_in, b_in, dw, dw_b, ln_g, ln_b, w_out, b_out, mix_g, mix_b, mlp_w1, mlp_w2):
    b, s, d = x.shape
    f = mlp_w1.shape[2]
    tm = TOKEN_TILE
    tiles = s // tm
    steps = b * tiles
    tile = pl.BlockSpec((1, tm, d), lambda bi, ti: (bi, ti, 0))
    w1_rows, w2_rows = d // steps, f // steps
    return pl.pallas_call(
        functools.partial(_conv_kernel, j, i),
        grid=(b, tiles),
        in_specs=[tile, _layer_block(j, (d, 2 * d)), _whole(b_in), _layer_block(j, dw.shape[1:]),
                  _whole(dw_b), _whole(ln_g), _whole(ln_b), _layer_block(j, (d, d)), _whole(b_out),
                  _whole(mix_g), _whole(mix_b),
                  pl.BlockSpec((None, w1_rows, f), lambda bi, ti: (i, bi * tiles + ti, 0)),
                  pl.BlockSpec((None, w2_rows, d), lambda bi, ti: (i, bi * tiles + ti, 0))],
        out_specs=[tile,
                   pl.BlockSpec((w1_rows, f), lambda bi, ti: (bi * tiles + ti, 0)),
                   pl.BlockSpec((w2_rows, d), lambda bi, ti: (bi * tiles + ti, 0))],
        out_shape=[jax.ShapeDtypeStruct(x.shape, x.dtype),
                   jax.ShapeDtypeStruct((d, f), jnp.bfloat16),
                   jax.ShapeDtypeStruct((f, d), jnp.bfloat16)],
        scratch_shapes=[pltpu.VMEM((d // LANES, CONV_HALO + tm, LANES), jnp.float32)],
        compiler_params=_params("arbitrary", "arbitrary"),
        name="conv_module",
    )(x, w_in, b_in, dw, dw_b, ln_g, ln_b, w_out, b_out, mix_g, mix_b, mlp_w1, mlp_w2)


def kernel(x, pool_w, pool_scale, conv_w_in, conv_b_in, conv_dw, conv_dw_b, conv_ln_g, conv_ln_b,
           conv_w_out, conv_b_out, mix_ln_g, mix_ln_b, mlp_w1, mlp_b1, mlp_w2, mlp_b2, mlp_ln_g, mlp_ln_b):
    for i in range(DEPTH):
        j = i // 2
        if i % 2 == 0:
            x = _pool_mlp_layer(x, j, i, pool_w, pool_scale, mix_ln_g, mix_ln_b,
                                mlp_w1, mlp_b1, mlp_w2, mlp_b2, mlp_ln_g, mlp_ln_b)
        else:
            x, w1c, w2c = _conv_layer(x, j, i, conv_w_in, conv_b_in, conv_dw, conv_dw_b, conv_ln_g,
                                      conv_ln_b, conv_w_out, conv_b_out, mix_ln_g, mix_ln_b,
                                      mlp_w1, mlp_w2)
            x = _mlp_layer(x, i, w1c, mlp_b1, w2c, mlp_b2, mlp_ln_g, mlp_ln_b)
    return x
```
